```python
import math
import jax, jax.numpy as jnp
from jax import lax
import numpy as np

D_MODEL = 1024
BATCH = 4
SEQ = 4096
DEPTH = 4
DEC_BATCH = 4
DEC_SEQ = 8192
PAST_LEN = 128

CHUNK = 128
EPS = 1e-6
N_BRANCH = 3
BRANCH_DIM = 1024
SSD_HEADS = 16
SSD_HEAD_DIM = 64
SSD_DIM = SSD_HEADS * SSD_HEAD_DIM
SSD_GROUPS = 2
SSD_STATE = 128
SSD_CONV = 4
SSD_CONV_CH = SSD_DIM + 2 * SSD_GROUPS * SSD_STATE
ML_HEADS = 4
ML_QK = 128
ML_V = 256
ML_QK_DIM = ML_HEADS * ML_QK
ML_V_DIM = ML_HEADS * ML_V
RET_HEADS = 4
RET_QK = 128
RET_V = 256
RET_QK_DIM = RET_HEADS * RET_QK
RET_V_DIM = RET_HEADS * RET_V
ROPE_BASE = 10000.0
FFN_DIM = -(-8 * D_MODEL // (3 * 256)) * 256
IN_SIZES = (SSD_DIM, SSD_CONV_CH, 2 * SSD_HEADS,
            ML_QK_DIM, ML_QK_DIM, ML_V_DIM, ML_V_DIM, 4 * ML_HEADS,
            RET_QK_DIM, RET_QK_DIM, RET_V_DIM, RET_V_DIM)
IN_COLS = sum(IN_SIZES)

kernel_name = "hybrid_bidir_ssd_mlstm_retention_encoder"

F32 = jnp.float32


def split_cols(t, sizes):
    out, start = [], 0
    for n in sizes:
        out.append(t[..., start:start + n])
        start += n
    return out


def _flip(t):
    return jnp.flip(t, axis=1)


def rms_norm(x, g):
    xf = x.astype(F32)
    y = xf * lax.rsqrt(jnp.mean(xf * xf, axis=-1, keepdims=True) + EPS)
    return (y * g.astype(F32)).astype(x.dtype)


def head_rms_norm(y, g):
    y = y * lax.rsqrt(jnp.mean(y * y, axis=-1, keepdims=True) + EPS)
    return y.reshape(*y.shape[:-2], -1) * g.astype(F32)


def centred_conv(u, w, b):
    s = u.shape[1]
    left = SSD_CONV // 2
    right = SSD_CONV - 1 - left
    up = jnp.pad(u, ((0, 0), (left, right), (0, 0)))
    out = b
    for t in range(SSD_CONV):
        out = out + up[:, t:t + s] * w[t]
    return out


def rotary(t, pos):
    half = t.shape[-1] // 2
    inv = ROPE_BASE ** (-jnp.arange(half, dtype=F32) / half)
    ang = pos[:, None] * inv
    cos = jnp.cos(ang)[None, :, None, :]
    sin = jnp.sin(ang)[None, :, None, :]
    t1, t2 = t[..., :half], t[..., half:]
    return jnp.concatenate([t1 * cos - t2 * sin, t1 * sin + t2 * cos], axis=-1)


def ssd_scan(x, dt, a, bm, cm):
    bsz, s = x.shape[0], x.shape[1]
    nc = s // CHUNK
    e = SSD_HEADS // SSD_GROUPS
    x = x.reshape(bsz, nc, CHUNK, SSD_GROUPS, e, SSD_HEAD_DIM)
    dt = dt.reshape(bsz, nc, CHUNK, SSD_GROUPS, e)
    bm = bm.reshape(bsz, nc, CHUNK, SSD_GROUPS, SSD_STATE)
    cm = cm.reshape(bsz, nc, CHUNK, SSD_GROUPS, SSD_STATE)
    cs = jnp.cumsum(dt * a.reshape(SSD_GROUPS, e), axis=2)
    xdt = x * dt[..., None]
    tri = jnp.tril(jnp.ones((CHUNK, CHUNK), dtype=bool))[:, :, None, None]
    seg = cs[:, :, :, None] - cs[:, :, None, :]
    decay = jnp.exp(jnp.where(tri, seg, -jnp.inf))
    cb = jnp.einsum('bclgn,bcsgn->bclsg', cm, bm)
    y_diag = jnp.einsum('bclsge,bcsgep->bclgep', cb[..., None] * decay, xdt)
    to_end = jnp.exp(cs[:, :, -1:] - cs)
    states = jnp.einsum('bclgn,bclge,bclgep->bcgepn', bm, to_end, xdt)
    chunk_decay = jnp.exp(cs[:, :, -1])

    def step(h, inp):
        st, dec = inp
        return h * dec[..., None, None] + st, h

    h0 = jnp.zeros((bsz, SSD_GROUPS, e, SSD_HEAD_DIM, SSD_STATE), F32)
    _, prev = lax.scan(step, h0, (jnp.moveaxis(states, 1, 0), jnp.moveaxis(chunk_decay, 1, 0)))
    prev = jnp.moveaxis(prev, 0, 1)
    y_off = jnp.einsum('bclgn,bcgepn,bclge->bclgep', cm, prev, jnp.exp(cs))
    return (y_diag + y_off).reshape(bsz, s, SSD_HEADS, SSD_HEAD_DIM)


def ssd_mixer(z, xbc, dt_raw, conv_w, conv_b, a_log, dt_bias, d_skip, norm_g):
    dtype = z.dtype
    bsz, s = z.shape[0], z.shape[1]
    xbc = jax.nn.silu(centred_conv(xbc.astype(F32), conv_w.astype(F32), conv_b.astype(F32)))
    xs, bm, cm = split_cols(xbc, (SSD_DIM, SSD_GROUPS * SSD_STATE, SSD_GROUPS * SSD_STATE))
    xs = xs.reshape(bsz, s, SSD_HEADS, SSD_HEAD_DIM)
    bm = bm.reshape(bsz, s, SSD_GROUPS, SSD_STATE)
    cm = cm.reshape(bsz, s, SSD_GROUPS, SSD_STATE)
    dt = jax.nn.softplus(dt_raw.astype(F32).reshape(bsz, s, 2, SSD_HEADS) + dt_bias.astype(F32))
    a = -jnp.exp(a_log.astype(F32))
    y_f = ssd_scan(xs, dt[:, :, 0], a[0], bm, cm)
    y_b = _flip(ssd_scan(_flip(xs), _flip(dt[:, :, 1]), a[1], _flip(bm), _flip(cm)))
    y = y_f + y_b + xs * d_skip.astype(F32)[:, None]
    y = y.reshape(bsz, s, SSD_DIM) * jax.nn.silu(z.astype(F32))
    return rms_norm(y, norm_g).astype(dtype)


def mlstm_scan(q, k, v, i_pre, f_pre):
    bsz, s = q.shape[0], q.shape[1]
    nc = s // CHUNK

    def chunks(t):
        return jnp.moveaxis(t.reshape(bsz, nc, CHUNK, *t.shape[2:]), 1, 0)

    tri = jnp.tril(jnp.ones((CHUNK, CHUNK), dtype=bool))[None, :, :, None]

    def step(carry, inp):
        c_st, n_st, m_st = carry
        qq, kk, vv, ii, lf = inp
        bcum = jnp.cumsum(lf, axis=1)
        logd = jnp.where(tri, bcum[:, :, None] - bcum[:, None] + ii[:, None], -jnp.inf)
        inter = bcum + m_st[:, None]
        m_t = jnp.maximum(inter, jnp.max(logd, axis=2))
        w = jnp.exp(logd - m_t[:, :, None])
        w_inter = jnp.exp(inter - m_t)
        scores = jnp.einsum('bthd,bshd->btsh', qq, kk) * w
        num = (jnp.einsum('btsh,bshv->bthv', scores, vv)
               + w_inter[..., None] * jnp.einsum('bthd,bhdv->bthv', qq, c_st))
        den = jnp.sum(scores, axis=2) + w_inter * jnp.einsum('bthd,bhd->bth', qq, n_st)
        h = num / jnp.maximum(jnp.abs(den), jnp.exp(-m_t))[..., None]
        btot = bcum[:, -1]
        log_end = btot[:, None] - bcum + ii
        m_new = jnp.maximum(btot + m_st, jnp.max(log_end, axis=1))
        w_end = jnp.exp(log_end - m_new[:, None])
        dec = jnp.exp(btot + m_st - m_new)
        c_new = dec[..., None, None] * c_st + jnp.einsum('blh,blhd,blhv->bhdv', w_end, kk, vv)
        n_new = dec[..., None] * n_st + jnp.einsum('blh,blhd->bhd', w_end, kk)
        return (c_new, n_new, m_new), h

    carry0 = (jnp.zeros((bsz, ML_HEADS, ML_QK, ML_V), F32),
              jnp.zeros((bsz, ML_HEADS, ML_QK), F32),
              jnp.zeros((bsz, ML_HEADS), F32))
    xs = (chunks(q), chunks(k), chunks(v), chunks(i_pre), chunks(jax.nn.log_sigmoid(f_pre)))
    _, h = lax.scan(step, carry0, xs)
    return jnp.moveaxis(h, 0, 1).reshape(bsz, s, ML_HEADS, ML_V)


def mlstm_mixer(q, k, v, o, gates, gate_bias, norm_g):
    dtype = o.dtype
    bsz, s = q.shape[0], q.shape[1]
    q = q.astype(F32).reshape(bsz, s, ML_HEADS, ML_QK) * (ML_QK ** -0.5)
    k = k.astype(F32).reshape(bsz, s, ML_HEADS, ML_QK)
    v = v.astype(F32).reshape(bsz, s, ML_HEADS, ML_V)
    g = gates.astype(F32).reshape(bsz, s, 4, ML_HEADS) + gate_bias.astype(F32)
    h_f = mlstm_scan(q, k, v, g[:, :, 0], g[:, :, 1])
    h_b = _flip(mlstm_scan(_flip(q), _flip(k), _flip(v), _flip(g[:, :, 2]), _flip(g[:, :, 3])))
    h = head_rms_norm(h_f + h_b, norm_g)
    return (jax.nn.sigmoid(o.astype(F32)) * h).astype(dtype)


def retention_scan(q, k, v, log_gamma, include_diag):
    bsz, s = q.shape[0], q.shape[1]
    nc = s // CHUNK
    qc = q.reshape(bsz, nc, CHUNK, RET_HEADS, RET_QK)
    kc = k.reshape(bsz, nc, CHUNK, RET_HEADS, RET_QK)
    vc = v.reshape(bsz, nc, CHUNK, RET_HEADS, RET_V)
    pos = jnp.arange(CHUNK, dtype=F32)
    dist = pos[:, None] - pos[None, :]
    mask = (dist >= 0) if include_diag else (dist > 0)
    dmat = jnp.where(mask[:, :, None], jnp.exp(jnp.maximum(dist, 0.0)[:, :, None] * log_gamma), 0.0)
    scores = jnp.einsum('bclhd,bcshd->bclsh', qc, kc) * dmat
    y_intra = jnp.einsum('bclsh,bcshv->bclhv', scores, vc)
    zeta = jnp.exp((CHUNK - 1 - pos)[:, None] * log_gamma)
    xi = jnp.exp((pos + 1)[:, None] * log_gamma)
    states = jnp.einsum('bcshd,sh,bcshv->bchdv', kc, zeta, vc)
    chunk_decay = jnp.exp(CHUNK * log_gamma)

    def step(r, st):
        return r * chunk_decay[:, None, None] + st, r

    r0 = jnp.zeros((bsz, RET_HEADS, RET_QK, RET_V), F32)
    _, prev = lax.scan(step, r0, jnp.moveaxis(states, 1, 0))
    prev = jnp.moveaxis(prev, 0, 1)
    y_cross = jnp.einsum('bclhd,bchdv,lh->bclhv', qc, prev, xi)
    return (y_intra + y_cross).reshape(bsz, s, RET_HEADS, RET_V)


def retention_mixer(q, k, v, g, norm_g):
    dtype = g.dtype
    bsz, s = q.shape[0], q.shape[1]
    pos = jnp.arange(s, dtype=F32)
    q = rotary(q.astype(F32).reshape(bsz, s, RET_HEADS, RET_QK), pos)
    k = rotary(k.astype(F32).reshape(bsz, s, RET_HEADS, RET_QK), pos) * (RET_QK ** -0.5)
    v = v.astype(F32).reshape(bsz, s, RET_HEADS, RET_V)
    log_gamma = jnp.log(1.0 - jnp.exp(jnp.linspace(math.log(1.0 / 32.0), math.log(1.0 / 512.0), RET_HEADS, dtype=F32)))
    y = (retention_scan(q, k, v, log_gamma, True)
         + _flip(retention_scan(_flip(q), _flip(k), _flip(v), log_gamma, False)))
    y = head_rms_norm(y, norm_g)
    return (jax.nn.silu(g.astype(F32)) * y).astype(dtype)


def encoder_trunk(x, norm_mix_g, w_in, w_gate, b_gate, conv_w, conv_b, ssd_a_log, ssd_dt_bias,
                  ssd_d, ssd_norm_g, mlstm_gate_bias, mlstm_norm_g, ret_norm_g, w_branch, w_out,
                  norm_ffn_g, w_ffn_gate, w_ffn_up, w_ffn_down, final_norm_g):
    for layer in range(DEPTH):
        u = rms_norm(x, norm_mix_g[layer])
        (ssd_z, ssd_xbc, ssd_dt, ml_q, ml_k, ml_v, ml_o, ml_g,
         rt_q, rt_k, rt_v, rt_g) = split_cols(u @ w_in[layer], IN_SIZES)
        gate = jax.nn.sigmoid((u @ w_gate[layer] + b_gate[layer]).astype(F32))
        g_a, g_b, g_c = split_cols(gate, (D_MODEL, D_MODEL, D_MODEL))
        y_a = ssd_mixer(ssd_z, ssd_xbc, ssd_dt, conv_w[layer], conv_b[layer], ssd_a_log[layer],
                        ssd_dt_bias[layer], ssd_d[layer], ssd_norm_g[layer]) @ w_branch[layer, 0]
        y_b = mlstm_mixer(ml_q, ml_k, ml_v, ml_o, ml_g, mlstm_gate_bias[layer],
                          mlstm_norm_g[layer]) @ w_branch[layer, 1]
        y_c = retention_mixer(rt_q, rt_k, rt_v, rt_g, ret_norm_g[layer]) @ w_branch[layer, 2]
        merged = (g_a * y_a + g_b * y_b + g_c * y_c).astype(x.dtype)
        x = x + merged @ w_out[layer]
        h = rms_norm(x, norm_ffn_g[layer])
        x = x + (jax.nn.silu(h @ w_ffn_gate[layer]) * (h @ w_ffn_up[layer])) @ w_ffn_down[layer]
    return rms_norm(x, final_norm_g)


def setup_inputs(seed: int = 0) -> dict:
    key = jax.random.key(seed)
    ks = jax.random.split(key, 24)

    def nrm(k, shape, scale):
        return jax.random.normal(k, shape, F32) * scale

    x_prompt = nrm(ks[0], (BATCH, SEQ, D_MODEL), 1.0)
    x_sample = nrm(ks[1], (DEC_BATCH, DEC_SEQ, D_MODEL), 1.0)
    norm_mix_g = 1.0 + nrm(ks[2], (DEPTH, D_MODEL), 0.01)
    w_in = nrm(ks[3], (DEPTH, D_MODEL, IN_COLS), D_MODEL ** -0.5)
    w_gate = nrm(ks[4], (DEPTH, D_MODEL, N_BRANCH * D_MODEL), D_MODEL ** -0.5)
    b_gate = nrm(ks[5], (DEPTH, N_BRANCH * D_MODEL), 0.01)
    conv_w = nrm(ks[6], (DEPTH, SSD_CONV, SSD_CONV_CH), SSD_CONV ** -0.5)
    conv_b = nrm(ks[7], (DEPTH, SSD_CONV_CH), 0.01)
    ssd_a_log = jnp.log(jax.random.uniform(ks[8], (DEPTH, 2, SSD_HEADS), F32, 1.0, 16.0))
    dt0 = jnp.exp(jax.random.uniform(ks[9], (DEPTH, 2, SSD_HEADS), F32, math.log(1e-3), math.log(1e-1)))
    ssd_dt_bias = dt0 + jnp.log(-jnp.expm1(-dt0))
    ssd_d = 1.0 + nrm(ks[10], (DEPTH, SSD_HEADS), 0.1)
    ssd_norm_g = 1.0 + nrm(ks[11], (DEPTH, SSD_DIM), 0.01)
    f_base = jnp.linspace(3.0, 6.0, ML_HEADS, dtype=F32)
    gate_base = jnp.stack([jnp.zeros_like(f_base), f_base, jnp.zeros_like(f_base), f_base])
    mlstm_gate_bias = gate_base + nrm(ks[12], (DEPTH, 4, ML_HEADS), 0.1)
    mlstm_norm_g = 1.0 + nrm(ks[13], (DEPTH, ML_V_DIM), 0.01)
    ret_norm_g = 1.0 + nrm(ks[14], (DEPTH, RET_V_DIM), 0.01)
    w_branch = nrm(ks[15], (DEPTH, N_BRANCH, BRANCH_DIM, D_MODEL), BRANCH_DIM ** -0.5)
    w_out = nrm(ks[16], (DEPTH, D_MODEL, D_MODEL), D_MODEL ** -0.5)
    norm_ffn_g = 1.0 + nrm(ks[17], (DEPTH, D_MODEL), 0.01)
    w_ffn_gate = nrm(ks[18], (DEPTH, D_MODEL, FFN_DIM), D_MODEL ** -0.5)
    w_ffn_up = nrm(ks[19], (DEPTH, D_MODEL, FFN_DIM), D_MODEL ** -0.5)
    w_ffn_down = nrm(ks[20], (DEPTH, FFN_DIM, D_MODEL), FFN_DIM ** -0.5)
    final_norm_g = 1.0 + nrm(ks[21], (D_MODEL,), 0.01)
    return {"x_prompt": x_prompt, "x_sample": x_sample, "norm_mix_g": norm_mix_g, "w_in": w_in,
            "w_gate": w_gate, "b_gate": b_gate, "conv_w": conv_w, "conv_b": conv_b,
            "ssd_a_log": ssd_a_log, "ssd_dt_bias": ssd_dt_bias, "ssd_d": ssd_d,
            "ssd_norm_g": ssd_norm_g, "mlstm_gate_bias": mlstm_gate_bias,
            "mlstm_norm_g": mlstm_norm_g, "ret_norm_g": ret_norm_g, "w_branch": w_branch,
            "w_out": w_out, "norm_ffn_g": norm_ffn_g, "w_ffn_gate": w_ffn_gate,
            "w_ffn_up": w_ffn_up, "w_ffn_down": w_ffn_down, "final_norm_g": final_norm_g}


def reference(x_prompt, x_sample, norm_mix_g, w_in, w_gate, b_gate, conv_w, conv_b, ssd_a_log,
              ssd_dt_bias, ssd_d, ssd_norm_g, mlstm_gate_bias, mlstm_norm_g, ret_norm_g, w_branch,
              w_out, norm_ffn_g, w_ffn_gate, w_ffn_up, w_ffn_down, final_norm_g):
    y_prompt = encoder_trunk(x_prompt, norm_mix_g, w_in, w_gate, b_gate, conv_w, conv_b, ssd_a_log,
                             ssd_dt_bias, ssd_d, ssd_norm_g, mlstm_gate_bias, mlstm_norm_g, ret_norm_g,
                             w_branch, w_out, norm_ffn_g, w_ffn_gate, w_ffn_up, w_ffn_down, final_norm_g)
    y_sample = encoder_trunk(x_sample, norm_mix_g, w_in, w_gate, b_gate, conv_w, conv_b, ssd_a_log,
                             ssd_dt_bias, ssd_d, ssd_norm_g, mlstm_gate_bias, mlstm_norm_g, ret_norm_g,
                             w_branch, w_out, norm_ffn_g, w_ffn_gate, w_ffn_up, w_ffn_down, final_norm_g)
    return (y_prompt, y_sample)
```

```python
import functools
import math

import numpy as np
import jax
import jax.numpy as jnp
from jax import lax
from jax.experimental import pallas as pl
from jax.experimental.pallas import tpu as pltpu

F32 = jnp.float32
BF16 = jnp.bfloat16

D_MODEL = 1024
CHUNK = 128
EPS = 1e-6
SSD_HEADS = 16
SSD_HEAD_DIM = 64
SSD_DIM = SSD_HEADS * SSD_HEAD_DIM
SSD_GROUPS = 2
SSD_STATE = 128
SSD_CONV = 4
SSD_BC = 2 * SSD_GROUPS * SSD_STATE
SSD_CONV_CH = SSD_DIM + SSD_BC
ML_HEADS = 4
ML_QK = 128
ML_V = 256
RET_HEADS = 4
RET_QK = 128
RET_V = 256
ROPE_BASE = 10000.0
IN_SIZES = (SSD_DIM, SSD_CONV_CH, 2 * SSD_HEADS,
            ML_HEADS * ML_QK, ML_HEADS * ML_QK, ML_HEADS * ML_V, ML_HEADS * ML_V, 4 * ML_HEADS,
            RET_HEADS * RET_QK, RET_HEADS * RET_QK, RET_HEADS * RET_V, RET_HEADS * RET_V)

COL_Z, COL_MLV, COL_MLO, COL_RV, COL_RG, COL_XS = 0, 1024, 2048, 3072, 4096, 5120
COL_MLQ, COL_MLK, COL_RQ, COL_RK, COL_BC = 6144, 6656, 7168, 7680, 8192
MAIN_COLS = 8704
SMALL_COLS = 128
SMALL_GATE0 = 2 * SSD_HEADS

V7X_VMEM_BUDGET = 56 * 1024 * 1024
CONV_HALO = 16

_lin = np.linspace(math.log(1.0 / 32.0), math.log(1.0 / 512.0), RET_HEADS, dtype=np.float32)
RET_LOG_GAMMA = [float(v) for v in np.log(np.float32(1.0) - np.exp(_lin)).astype(np.float32)]


def _vmem_limit(nbytes):
    return int(min(V7X_VMEM_BUDGET, max(32 * 1024 * 1024, 2 * nbytes)))


def _silu(x):
    return x * (1.0 / (1.0 + jnp.exp(-x)))


def _sigmoid(x):
    return 1.0 / (1.0 + jnp.exp(-x))


def _softplus(x):
    return jnp.maximum(x, 0.0) + jnp.log1p(jnp.exp(-jnp.abs(x)))


def _log_sigmoid(x):
    return -_softplus(-x)


def _rms(x, g):
    return x * lax.rsqrt(jnp.mean(x * x, axis=-1, keepdims=True) + EPS) * g


def _dot(a, b):
    return jnp.dot(a, b, preferred_element_type=F32)


def _dot_nt(a, b):
    return lax.dot_general(a, b, (((1,), (1,)), ((), ())), preferred_element_type=F32)


def _dot_tn(a, b):
    return lax.dot_general(a, b, (((0,), (0,)), ((), ())), preferred_element_type=F32)


def _split3(v):
    hi = v.astype(BF16)
    r = v - hi.astype(F32)
    mid = r.astype(BF16)
    lo = (r - mid.astype(F32)).astype(BF16)
    return hi, mid, lo


def _dot_exact_r(sel, v):
    hi, mid, lo = _split3(v)
    return _dot(sel, hi) + _dot(sel, mid) + _dot(sel, lo)


def _dot_exact_l(v, sel):
    hi, mid, lo = _split3(v)
    return _dot(hi, sel) + _dot(mid, sel) + _dot(lo, sel)


def _tri_masks(n):
    r = lax.broadcasted_iota(jnp.int32, (n, n), 0)
    c = lax.broadcasted_iota(jnp.int32, (n, n), 1)
    return r >= c, c >= r


def _norm_matmul_kernel(x_ref, g_ref, w_ref, b_ref, o_ref, u_ref, *, act):
    @pl.when(pl.program_id(1) == 0)
    def _():
        u_ref[...] = _rms(x_ref[...], g_ref[...]).astype(BF16)

    acc = _dot(u_ref[...], w_ref[...]) + b_ref[...]
    if act == "sigmoid":
        acc = _sigmoid(acc)
    o_ref[...] = acc.astype(o_ref.dtype)


def norm_matmul(x2d, g, w, b, act, out_dtype, tm, tn):
    t, d = x2d.shape
    n = w.shape[1]
    tm = min(tm, t)
    est = 2 * tm * d * 4 + tm * d * 2 + 2 * d * tn * 2 + 2 * tm * tn * 4
    return pl.pallas_call(
        functools.partial(_norm_matmul_kernel, act=act),
        grid=(t // tm, n // tn),
        in_specs=[pl.BlockSpec((tm, d), lambda i, j: (i, 0)),
                  pl.BlockSpec((1, d), lambda i, j: (0, 0)),
                  pl.BlockSpec((d, tn), lambda i, j: (0, j)),
                  pl.BlockSpec((1, tn), lambda i, j: (0, j))],
        out_specs=pl.BlockSpec((tm, tn), lambda i, j: (i, j)),
        out_shape=jax.ShapeDtypeStruct((t, n), out_dtype),
        scratch_shapes=[pltpu.VMEM((tm, d), BF16)],
        compiler_params=pltpu.CompilerParams(dimension_semantics=("parallel", "arbitrary"),
                                             vmem_limit_bytes=_vmem_limit(est)),
        name="norm_matmul_" + act,
    )(x2d, g, w, b)


def _small_proj_kernel(x_ref, g_ref, w_ref, wt_ref, o_ref, ot_ref):
    u = _rms(x_ref[...], g_ref[...]).astype(BF16)
    o_ref[...] = _dot(u, w_ref[...])
    ot_ref[...] = _dot_nt(wt_ref[...], u)


def small_proj(x2d, g, w_small, tm):
    t, d = x2d.shape
    tm = min(tm, t)
    return pl.pallas_call(
        _small_proj_kernel,
        grid=(t // tm,),
        in_specs=[pl.BlockSpec((tm, d), lambda i: (i, 0)),
                  pl.BlockSpec((1, d), lambda i: (0, 0)),
                  pl.BlockSpec((d, SMALL_COLS), lambda i: (0, 0)),
                  pl.BlockSpec((SMALL_COLS, d), lambda i: (0, 0))],
        out_specs=[pl.BlockSpec((tm, SMALL_COLS), lambda i: (i, 0)),
                   pl.BlockSpec((SMALL_COLS, tm), lambda i: (0, i))],
        out_shape=[jax.ShapeDtypeStruct((t, SMALL_COLS), F32),
                   jax.ShapeDtypeStruct((SMALL_COLS, t), F32)],
        compiler_params=pltpu.CompilerParams(dimension_semantics=("parallel",),
                                             vmem_limit_bytes=_vmem_limit(3 * tm * d * 4)),
        name="small_proj",
    )(x2d, g, w_small, w_small.T)


def _conv_kernel(xs_ref, bc_ref, xs_p, bc_p, xs_n, bc_n, w_ref, b_ref, o_ref, ext_ref, *, tc, nt):
    t = pl.program_id(1)
    h = CONV_HALO
    has_prev = t > 0
    has_next = t < nt - 1
    zeros_x = jnp.zeros((h, SSD_DIM), F32)
    zeros_b = jnp.zeros((h, SSD_BC), F32)
    ext_ref[0:h, 0:SSD_DIM] = jnp.where(has_prev, xs_p[0].astype(F32), zeros_x)
    ext_ref[0:h, SSD_DIM:] = jnp.where(has_prev, bc_p[0].astype(F32), zeros_b)
    ext_ref[h:h + tc, 0:SSD_DIM] = xs_ref[0].astype(F32)
    ext_ref[h:h + tc, SSD_DIM:] = bc_ref[0].astype(F32)
    ext_ref[h + tc:, 0:SSD_DIM] = jnp.where(has_next, xs_n[0].astype(F32), zeros_x)
    ext_ref[h + tc:, SSD_DIM:] = jnp.where(has_next, bc_n[0].astype(F32), zeros_b)
    left = SSD_CONV // 2
    acc = jnp.broadcast_to(b_ref[...], (tc, SSD_CONV_CH))
    for tap in range(SSD_CONV):
        acc = acc + ext_ref[pl.ds(h - left + tap, tc), :] * w_ref[tap:tap + 1, :]
    o_ref[0] = _silu(acc).astype(o_ref.dtype)


def conv_silu(main, conv_w, conv_b, tc):
    bsz, s, _ = main.shape
    tc = min(tc, s)
    nt = s // tc
    hb = tc // CONV_HALO
    last_hb = s // CONV_HALO - 1
    cx, cb = COL_XS // SSD_DIM, COL_BC // SSD_BC
    prev_map = lambda b, t: (b, jnp.maximum(t * hb - 1, 0))
    next_map = lambda b, t: (b, jnp.minimum((t + 1) * hb, last_hb))
    est = 2 * tc * SSD_CONV_CH * 2 * 2 + (tc + 2 * CONV_HALO) * SSD_CONV_CH * 4 * 3
    return pl.pallas_call(
        functools.partial(_conv_kernel, tc=tc, nt=nt),
        grid=(bsz, nt),
        in_specs=[pl.BlockSpec((1, tc, SSD_DIM), lambda b, t: (b, t, cx)),
                  pl.BlockSpec((1, tc, SSD_BC), lambda b, t: (b, t, cb)),
                  pl.BlockSpec((1, CONV_HALO, SSD_DIM), lambda b, t: prev_map(b, t) + (cx,)),
                  pl.BlockSpec((1, CONV_HALO, SSD_BC), lambda b, t: prev_map(b, t) + (cb,)),
                  pl.BlockSpec((1, CONV_HALO, SSD_DIM), lambda b, t: next_map(b, t) + (cx,)),
                  pl.BlockSpec((1, CONV_HALO, SSD_BC), lambda b, t: next_map(b, t) + (cb,)),
                  pl.BlockSpec((SSD_CONV, SSD_CONV_CH), lambda b, t: (0, 0)),
                  pl.BlockSpec((1, SSD_CONV_CH), lambda b, t: (0, 0))],
        out_specs=pl.BlockSpec((1, tc, SSD_CONV_CH), lambda b, t: (b, t, 0)),
        out_shape=jax.ShapeDtypeStruct((bsz, s, SSD_CONV_CH), BF16),
        scratch_shapes=[pltpu.VMEM((tc + 2 * CONV_HALO, SSD_CONV_CH), F32)],
        compiler_params=pltpu.CompilerParams(dimension_semantics=("parallel", "parallel"),
                                             vmem_limit_bytes=_vmem_limit(est)),
        name="conv_silu",
    )(main, main, main, main, main, main, conv_w, conv_b)


def _ssd_dir(d, xbc, pre, pre_t, a_row, a_col, p_ref, y_ref, lower, upper):
    n = CHUNK
    cum_mask, cum_mask_t = (lower, upper) if d == 0 else (upper, lower)
    tri = cum_mask.astype(BF16)
    tri_t = cum_mask_t.astype(BF16)
    last = n - 1 if d == 0 else 0
    c0 = d * SSD_HEADS

    dt = _softplus(pre)
    dt_t = _softplus(pre_t)
    cs = _dot_exact_r(tri, dt * a_row)
    cs_t = _dot_exact_l(dt_t * a_col, tri_t)
    cs_last = cs[last:last + 1, :]
    w_state = dt * jnp.exp(cs_last - cs)
    e_cs = jnp.exp(cs)

    er = lax.broadcasted_iota(jnp.int32, (SMALL_COLS, SSD_DIM), 0)
    ec = lax.broadcasted_iota(jnp.int32, (SMALL_COLS, SSD_DIM), 1)
    expand = (er == c0 + ec // SSD_HEAD_DIM).astype(BF16)
    w_state_x = _dot_exact_l(w_state, expand)
    e_cs_x = _dot_exact_l(e_cs, expand)
    chunk_decay_x = e_cs_x[last:last + 1, :]

    xs = xbc[:, 0:SSD_DIM]
    xw = (xs.astype(F32) * w_state_x).astype(BF16)
    lane = lax.broadcasted_iota(jnp.int32, (n, 2 * SSD_HEAD_DIM), 1)
    hpg = SSD_HEADS // SSD_GROUPS
    gw = hpg * SSD_HEAD_DIM
    for g in range(SSD_GROUPS):
        bm = xbc[:, SSD_DIM + g * SSD_STATE:SSD_DIM + (g + 1) * SSD_STATE]
        cm = xbc[:, SSD_DIM + (SSD_GROUPS + g) * SSD_STATE:SSD_DIM + (SSD_GROUPS + g + 1) * SSD_STATE]
        cb = _dot_nt(cm, bm)
        p_prev = p_ref[d, g]
        y_off = _dot(cm, p_prev.astype(BF16)) * e_cs_x[:, g * gw:(g + 1) * gw]
        p_ref[d, g] = p_prev * chunk_decay_x[:, g * gw:(g + 1) * gw] + _dot_tn(bm, xw[:, g * gw:(g + 1) * gw])
        for j in range(hpg // 2):
            ms = []
            for hh in (g * hpg + 2 * j, g * hpg + 2 * j + 1):
                col = c0 + hh
                seg = cs[:, col:col + 1] - cs_t[col:col + 1, :]
                dec = jnp.exp(jnp.where(cum_mask, seg, -jnp.inf))
                ms.append((cb * (dec * dt_t[col:col + 1, :])).astype(BF16))
            lhs = jnp.concatenate(ms, axis=1)
            h0 = g * hpg + 2 * j
            xp = xs[:, h0 * SSD_HEAD_DIM:(h0 + 2) * SSD_HEAD_DIM]
            zero = jnp.zeros_like(xp)
            rhs = jnp.concatenate([jnp.where(lane < SSD_HEAD_DIM, xp, zero),
                                   jnp.where(lane >= SSD_HEAD_DIM, xp, zero)], axis=0)
            lo = 2 * j * SSD_HEAD_DIM
            y_ref[0, :, h0 * SSD_HEAD_DIM:(h0 + 2) * SSD_HEAD_DIM] = (
                _dot(lhs, rhs) + y_off[:, lo:lo + 2 * SSD_HEAD_DIM])


def _ssd_kernel(xf_ref, xb_ref, sf_ref, sb_ref, stf_ref, stb_ref, prow_ref, pcol_ref,
                yf_ref, yb_ref, p_ref):
    @pl.when(pl.program_id(1) == 0)
    def _():
        p_ref[...] = jnp.zeros_like(p_ref)

    lower, upper = _tri_masks(CHUNK)
    lane_r = lax.broadcasted_iota(jnp.int32, (1, SMALL_COLS), 1)
    sub_c = lax.broadcasted_iota(jnp.int32, (SMALL_COLS, 1), 0)
    a_row = jnp.where(lane_r < 2 * SSD_HEADS, -jnp.exp(prow_ref[1:2, :]), 0.0)
    a_col = jnp.where(sub_c < 2 * SSD_HEADS, -jnp.exp(pcol_ref[:, 1:2]), 0.0)
    bias_row = prow_ref[0:1, :]
    bias_col = pcol_ref[:, 0:1]
    _ssd_dir(0, xf_ref[0], sf_ref[0] + bias_row, stf_ref[...] + bias_col, a_row, a_col,
             p_ref, yf_ref, lower, upper)
    _ssd_dir(1, xb_ref[0], sb_ref[0] + bias_row, stb_ref[...] + bias_col, a_row, a_col,
             p_ref, yb_ref, lower, upper)


def _scan_maps(nc):
    fwd3 = lambda b, c: (b, c, 0)
    bwd3 = lambda b, c: (b, nc - 1 - c, 0)
    fwd_t = lambda b, c: (0, b * nc + c)
    bwd_t = lambda b, c: (0, b * nc + nc - 1 - c)
    return fwd3, bwd3, fwd_t, bwd_t


def ssd_scan(xbc, small, small_t, par_row, par_col):
    bsz, s, _ = xbc.shape
    nc = s // CHUNK
    fwd3, bwd3, fwd_t, bwd_t = _scan_maps(nc)
    hpg = SSD_HEADS // SSD_GROUPS
    out = jax.ShapeDtypeStruct((bsz, s, SSD_DIM), F32)
    return pl.pallas_call(
        _ssd_kernel,
        grid=(bsz, nc),
        in_specs=[pl.BlockSpec((1, CHUNK, SSD_CONV_CH), fwd3),
                  pl.BlockSpec((1, CHUNK, SSD_CONV_CH), bwd3),
                  pl.BlockSpec((1, CHUNK, SMALL_COLS), fwd3),
                  pl.BlockSpec((1, CHUNK, SMALL_COLS), bwd3),
                  pl.BlockSpec((SMALL_COLS, CHUNK), fwd_t),
                  pl.BlockSpec((SMALL_COLS, CHUNK), bwd_t),
                  pl.BlockSpec((8, SMALL_COLS), lambda b, c: (0, 0)),
                  pl.BlockSpec((SMALL_COLS, 8), lambda b, c: (0, 0))],
        out_specs=[pl.BlockSpec((1, CHUNK, SSD_DIM), fwd3),
                   pl.BlockSpec((1, CHUNK, SSD_DIM), bwd3)],
        out_shape=[out, out],
        scratch_shapes=[pltpu.VMEM((2, SSD_GROUPS, SSD_STATE, hpg * SSD_HEAD_DIM), F32)],
        compiler_params=pltpu.CompilerParams(dimension_semantics=("parallel", "arbitrary"),
                                             vmem_limit_bytes=_vmem_limit(16 * 1024 * 1024)),
        name="ssd_scan",
    )(xbc, xbc, small, small, small_t, small_t, par_row, par_col)


def _mlstm_dir(d, q_ref, k_ref, v_ref, pre, pre_t, s_ref, m_ref, h_ref, lower, upper):
    n = CHUNK
    cum_mask, cum_mask_t = (lower, upper) if d == 0 else (upper, lower)
    tri = cum_mask.astype(BF16)
    tri_t = cum_mask_t.astype(BF16)
    last = n - 1 if d == 0 else 0
    lf = _log_sigmoid(pre)
    lf_t = _log_sigmoid(pre_t)
    bc = _dot_exact_r(tri, lf)
    bc_t = _dot_exact_l(lf_t, tri_t)
    ones = jnp.ones((n, ML_QK), BF16)
    for h in range(ML_HEADS):
        ci = SMALL_GATE0 + 2 * ML_HEADS * d + h
        cf = ci + ML_HEADS
        idx = d * ML_HEADS + h
        i_col = pre[:, ci:ci + 1]
        b_col = bc[:, cf:cf + 1]
        i_row = pre_t[ci:ci + 1, :]
        b_row = bc_t[cf:cf + 1, :]
        m_prev = m_ref[idx:idx + 1, 0:1]
        logd = jnp.where(cum_mask, b_col + (i_row - b_row), -jnp.inf)
        inter = b_col + m_prev
        m_t = jnp.maximum(inter, jnp.max(logd, axis=1, keepdims=True))
        w = jnp.exp(logd - m_t)
        w_inter = jnp.exp(inter - m_t)
        qh = (q_ref[0, :, h * ML_QK:(h + 1) * ML_QK].astype(F32) * (ML_QK ** -0.5)).astype(BF16)
        kh = k_ref[0, :, h * ML_QK:(h + 1) * ML_QK]
        vx = jnp.concatenate([v_ref[0, :, h * ML_V:(h + 1) * ML_V], ones], axis=1)
        scores = (_dot_nt(qh, kh) * w).astype(BF16)
        state = s_ref[idx]
        tot = _dot(scores, vx) + w_inter * _dot(qh, state.astype(BF16))
        den = jnp.maximum(jnp.abs(tot[:, ML_V:]), jnp.exp(-m_t))
        rinv = 1.0 / den
        h_ref[0, :, h * ML_V:(h + 1) * ML_V] = tot[:, 0:ML_V] * jnp.concatenate([rinv, rinv], axis=1)
        b_tot = b_col[last:last + 1, :]
        log_end = b_tot - b_col + i_col
        m_new = jnp.maximum(b_tot + m_prev, jnp.max(log_end, axis=0, keepdims=True))
        w_end = jnp.exp(log_end - m_new)
        dec = jnp.exp(b_tot + m_prev - m_new)
        kw = (kh.astype(F32) * w_end).astype(BF16)
        s_ref[idx] = dec * state + _dot_tn(kw, vx)
        m_ref[idx:idx + 1, :] = jnp.broadcast_to(m_new, (1, SMALL_COLS))


def _mlstm_kernel(qf_ref, kf_ref, vf_ref, qb_ref, kb_ref, vb_ref, sf_ref, sb_ref, stf_ref, stb_ref,
                  prow_ref, pcol_ref, hf_ref, hb_ref, s_ref, m_ref):
    @pl.when(pl.program_id(1) == 0)
    def _():
        s_ref[...] = jnp.zeros_like(s_ref)
        m_ref[...] = jnp.zeros_like(m_ref)

    lower, upper = _tri_masks(CHUNK)
    bias_row = prow_ref[0:1, :]
    bias_col = pcol_ref[:, 0:1]
    _mlstm_dir(0, qf_ref, kf_ref, vf_ref, sf_ref[0] + bias_row, stf_ref[...] + bias_col,
               s_ref, m_ref, hf_ref, lower, upper)
    _mlstm_dir(1, qb_ref, kb_ref, vb_ref, sb_ref[0] + bias_row, stb_ref[...] + bias_col,
               s_ref, m_ref, hb_ref, lower, upper)


def mlstm_scan(main, small, small_t, par_row, par_col):
    bsz, s, _ = main.shape
    nc = s // CHUNK
    fwd3, bwd3, fwd_t, bwd_t = _scan_maps(nc)
    qk_w, v_w = ML_HEADS * ML_QK, ML_HEADS * ML_V
    cq, ck, cv = COL_MLQ // qk_w, COL_MLK // qk_w, COL_MLV // v_w

    def col(m, cidx):
        return lambda b, c: m(b, c)[:2] + (cidx,)

    out = jax.ShapeDtypeStruct((bsz, s, v_w), F32)
    return pl.pallas_call(
        _mlstm_kernel,
        grid=(bsz, nc),
        in_specs=[pl.BlockSpec((1, CHUNK, qk_w), col(fwd3, cq)),
                  pl.BlockSpec((1, CHUNK, qk_w), col(fwd3, ck)),
                  pl.BlockSpec((1, CHUNK, v_w), col(fwd3, cv)),
                  pl.BlockSpec((1, CHUNK, qk_w), col(bwd3, cq)),
                  pl.BlockSpec((1, CHUNK, qk_w), col(bwd3, ck)),
                  pl.BlockSpec((1, CHUNK, v_w), col(bwd3, cv)),
                  pl.BlockSpec((1, CHUNK, SMALL_COLS), fwd3),
                  pl.BlockSpec((1, CHUNK, SMALL_COLS), bwd3),
                  pl.BlockSpec((SMALL_COLS, CHUNK), fwd_t),
                  pl.BlockSpec((SMALL_COLS, CHUNK), bwd_t),
                  pl.BlockSpec((8, SMALL_COLS), lambda b, c: (0, 0)),
                  pl.BlockSpec((SMALL_COLS, 8), lambda b, c: (0, 0))],
        out_specs=[pl.BlockSpec((1, CHUNK, v_w), fwd3),
                   pl.BlockSpec((1, CHUNK, v_w), bwd3)],
        out_shape=[out, out],
        scratch_shapes=[pltpu.VMEM((2 * ML_HEADS, ML_QK, ML_V + ML_QK), F32),
                        pltpu.VMEM((2 * ML_HEADS, SMALL_COLS), F32)],
        compiler_params=pltpu.CompilerParams(dimension_semantics=("parallel", "arbitrary"),
                                             vmem_limit_bytes=_vmem_limit(16 * 1024 * 1024)),
        name="mlstm_scan",
    )(main, main, main, main, main, main, small, small, small_t, small_t, par_row, par_col)


def _rotate(t, cos_f, sin_f):
    return t * cos_f + pltpu.roll(t, RET_QK // 2, 1) * sin_f


def _ret_dir(d, q_ref, k_ref, v_ref, cos_ref, sin_ref, r_ref, y_ref):
    n = CHUNK
    cos_f = cos_ref[...]
    sin_f = sin_ref[...]
    row = lax.broadcasted_iota(jnp.int32, (n, n), 0)
    colm = lax.broadcasted_iota(jnp.int32, (n, n), 1)
    dist = (row - colm).astype(F32)
    pos_v = lax.broadcasted_iota(jnp.int32, (n, RET_V), 0).astype(F32)
    pos_k = lax.broadcasted_iota(jnp.int32, (n, RET_QK), 0).astype(F32)
    for h in range(RET_HEADS):
        lg = RET_LOG_GAMMA[h]
        idx = d * RET_HEADS + h
        if d == 0:
            dmat = jnp.where(dist >= 0, jnp.exp(jnp.maximum(dist, 0.0) * lg), 0.0)
            xi = jnp.exp((pos_v + 1.0) * lg)
            zeta = jnp.exp((n - 1.0 - pos_k) * lg)
        else:
            dmat = jnp.where(dist < 0, jnp.exp(jnp.maximum(-dist, 0.0) * lg), 0.0)
            xi = jnp.exp((n - pos_v) * lg)
            zeta = jnp.exp(pos_k * lg)
        qf = _rotate(q_ref[0, :, h * RET_QK:(h + 1) * RET_QK].astype(F32), cos_f, sin_f)
        kf = _rotate(k_ref[0, :, h * RET_QK:(h + 1) * RET_QK].astype(F32), cos_f, sin_f) * (RET_QK ** -0.5)
        qh = qf.astype(BF16)
        kh = kf.astype(BF16)
        vh = v_ref[0, :, h * RET_V:(h + 1) * RET_V]
        scores = (_dot_nt(qh, kh) * dmat).astype(BF16)
        r_prev = r_ref[idx]
        y_ref[0, :, h * RET_V:(h + 1) * RET_V] = _dot(scores, vh) + _dot(qh, r_prev.astype(BF16)) * xi
        r_ref[idx] = r_prev * math.exp(n * lg) + _dot_tn((kf * zeta).astype(BF16), vh)


def _ret_kernel(qf_ref, kf_ref, vf_ref, cf_ref, nf_ref, qb_ref, kb_ref, vb_ref, cb_ref, nb_ref,
                yf_ref, yb_ref, r_ref):
    @pl.when(pl.program_id(1) == 0)
    def _():
        r_ref[...] = jnp.zeros_like(r_ref)

    _ret_dir(0, qf_ref, kf_ref, vf_ref, cf_ref, nf_ref, r_ref, yf_ref)
    _ret_dir(1, qb_ref, kb_ref, vb_ref, cb_ref, nb_ref, r_ref, yb_ref)


def retention_scan(main, cos_f, sin_f):
    bsz, s, _ = main.shape
    nc = s // CHUNK
    fwd3, bwd3, _, _ = _scan_maps(nc)
    qk_w, v_w = RET_HEADS * RET_QK, RET_HEADS * RET_V
    cq, ck, cv = COL_RQ // qk_w, COL_RK // qk_w, COL_RV // v_w

    def col(m, cidx):
        return lambda b, c: m(b, c)[:2] + (cidx,)

    tab_f = lambda b, c: (c, 0)
    tab_b = lambda b, c: (nc - 1 - c, 0)
    out = jax.ShapeDtypeStruct((bsz, s, v_w), F32)
    return pl.pallas_call(
        _ret_kernel,
        grid=(bsz, nc),
        in_specs=[pl.BlockSpec((1, CHUNK, qk_w), col(fwd3, cq)),
                  pl.BlockSpec((1, CHUNK, qk_w), col(fwd3, ck)),
                  pl.BlockSpec((1, CHUNK, v_w), col(fwd3, cv)),
                  pl.BlockSpec((CHUNK, RET_QK), tab_f),
                  pl.BlockSpec((CHUNK, RET_QK), tab_f),
                  pl.BlockSpec((1, CHUNK, qk_w), col(bwd3, cq)),
                  pl.BlockSpec((1, CHUNK, qk_w), col(bwd3, ck)),
                  pl.BlockSpec((1, CHUNK, v_w), col(bwd3, cv)),
                  pl.BlockSpec((CHUNK, RET_QK), tab_b),
                  pl.BlockSpec((CHUNK, RET_QK), tab_b)],
        out_specs=[pl.BlockSpec((1, CHUNK, v_w), fwd3),
                   pl.BlockSpec((1, CHUNK, v_w), bwd3)],
        out_shape=[out, out],
        scratch_shapes=[pltpu.VMEM((2 * RET_HEADS, RET_QK, RET_V), F32)],
        compiler_params=pltpu.CompilerParams(dimension_semantics=("parallel", "arbitrary"),
                                             vmem_limit_bytes=_vmem_limit(16 * 1024 * 1024)),
        name="retention_scan",
    )(main, main, main, cos_f, sin_f, main, main, main, cos_f, sin_f)


def _head_norm(y, nheads, width):
    parts = []
    for h in range(nheads):
        yh = y[:, h * width:(h + 1) * width]
        parts.append(yh * lax.rsqrt(jnp.mean(yh * yh, axis=-1, keepdims=True) + EPS))
    return jnp.concatenate(parts, axis=1)


def _post_mix_kernel(z_ref, o_ref, rg_ref, xs_ref, sf_ref, sb_ref, mf_ref, mb_ref, rf_ref, rb_ref,
                     gate_ref, x_ref, pg_ref, wb_ref, wo_ref, out_ref):
    d_skip, g_ssd, g_ml, g_ret = pg_ref[0:1, :], pg_ref[1:2, :], pg_ref[2:3, :], pg_ref[3:4, :]
    ya = (sf_ref[...] + sb_ref[...] + xs_ref[...].astype(F32) * d_skip) * _silu(z_ref[...].astype(F32))
    ya = _rms(ya, g_ssd).astype(BF16)
    yb = _head_norm(mf_ref[...] + mb_ref[...], ML_HEADS, ML_V) * g_ml
    yb = (_sigmoid(o_ref[...].astype(F32)) * yb).astype(BF16)
    yc = _head_norm(rf_ref[...] + rb_ref[...], RET_HEADS, RET_V) * g_ret
    yc = (_silu(rg_ref[...].astype(F32)) * yc).astype(BF16)
    gate = gate_ref[...].astype(F32)
    merged = (gate[:, 0:D_MODEL] * _dot(ya, wb_ref[0])
              + gate[:, D_MODEL:2 * D_MODEL] * _dot(yb, wb_ref[1])
              + gate[:, 2 * D_MODEL:] * _dot(yc, wb_ref[2]))
    out_ref[...] = x_ref[...] + _dot(merged.astype(BF16), wo_ref[...])


def post_mix(main2d, xbc2d, ssd_y, ml_h, ret_y, gate, x2d, pg, w_branch, w_out, tm):
    t, d = x2d.shape
    tm = min(tm, t)
    row = lambda i: (i, 0)
    blk = lambda c: pl.BlockSpec((tm, d), lambda i: (i, c))
    est = 2 * tm * d * (4 * 2 + 7 * 4 + 3 * 2 + 4) + 2 * 4 * d * d * 2
    return pl.pallas_call(
        _post_mix_kernel,
        grid=(t // tm,),
        in_specs=[blk(COL_Z // d), blk(COL_MLO // d), blk(COL_RG // d), blk(0),
                  blk(0), blk(0), blk(0), blk(0), blk(0), blk(0),
                  pl.BlockSpec((tm, 3 * d), row), blk(0),
                  pl.BlockSpec((8, d), lambda i: (0, 0)),
                  pl.BlockSpec((3, d, d), lambda i: (0, 0, 0)),
                  pl.BlockSpec((d, d), lambda i: (0, 0))],
        out_specs=pl.BlockSpec((tm, d), row),
        out_shape=jax.ShapeDtypeStruct((t, d), F32),
        compiler_params=pltpu.CompilerParams(dimension_semantics=("parallel",),
                                             vmem_limit_bytes=_vmem_limit(est)),
        name="post_mix",
    )(main2d, main2d, main2d, xbc2d, ssd_y[0], ssd_y[1], ml_h[0], ml_h[1], ret_y[0], ret_y[1],
      gate, x2d, pg, w_branch, w_out)


def _ffn_kernel(x_ref, g_ref, wg_ref, wu_ref, wd_ref, gf_ref, o_ref, h_ref, acc_ref, *, nf, final):
    j = pl.program_id(1)

    @pl.when(j == 0)
    def _():
        h_ref[...] = _rms(x_ref[...], g_ref[...]).astype(BF16)
        acc_ref[...] = jnp.zeros_like(acc_ref)

    h = h_ref[...]
    t = (_silu(_dot(h, wg_ref[...])) * _dot(h, wu_ref[...])).astype(BF16)
    acc_ref[...] += _dot(t, wd_ref[...])

    @pl.when(j == nf - 1)
    def _():
        y = x_ref[...] + acc_ref[...]
        if final:
            y = _rms(y, gf_ref[...])
        o_ref[...] = y


def ffn(x2d, g, w_gate, w_up, w_down, g_final, final, tm, tf):
    t, d = x2d.shape
    f = w_gate.shape[1]
    tm = min(tm, t)
    nf = f // tf
    est = 2 * 2 * tm * d * 4 + tm * d * 6 + 2 * 3 * d * tf * 2 + 3 * tm * tf * 4
    return pl.pallas_call(
        functools.partial(_ffn_kernel, nf=nf, final=final),
        grid=(t // tm, nf),
        in_specs=[pl.BlockSpec((tm, d), lambda i, j: (i, 0)),
                  pl.BlockSpec((1, d), lambda i, j: (0, 0)),
                  pl.BlockSpec((d, tf), lambda i, j: (0, j)),
                  pl.BlockSpec((d, tf), lambda i, j: (0, j)),
                  pl.BlockSpec((tf, d), lambda i, j: (j, 0)),
                  pl.BlockSpec((1, d), lambda i, j: (0, 0))],
        out_specs=pl.BlockSpec((tm, d), lambda i, j: (i, 0)),
        out_shape=jax.ShapeDtypeStruct((t, d), F32),
        scratch_shapes=[pltpu.VMEM((tm, d), BF16), pltpu.VMEM((tm, d), F32)],
        compiler_params=pltpu.CompilerParams(dimension_semantics=("parallel", "arbitrary"),
                                             vmem_limit_bytes=_vmem_limit(est)),
        name="ffn_final" if final else "ffn",
    )(x2d, g, w_gate, w_up, w_down, g_final)


def _prepare_layer(layer, w_in, w_gate, b_gate, conv_w, conv_b, ssd_a_log, ssd_dt_bias, ssd_d, ssd_norm_g,
                   mlstm_gate_bias, mlstm_norm_g, ret_norm_g, w_branch, w_out, w_ffn_gate, w_ffn_up,
                   w_ffn_down):
    offs = np.cumsum((0,) + IN_SIZES)
    w = w_in[layer]
    piece = lambda i: w[:, offs[i]:offs[i + 1]]
    z, xbc, dt, mlq, mlk, mlv, mlo, mlg, rq, rk, rv, rg = (piece(i) for i in range(len(IN_SIZES)))
    main = jnp.concatenate([z, mlv, mlo, rv, rg, xbc[:, :SSD_DIM], mlq, mlk, rq, rk, xbc[:, SSD_DIM:]],
                           axis=1).astype(BF16)
    pad = SMALL_COLS - dt.shape[1] - mlg.shape[1]
    small = jnp.concatenate([dt, mlg, jnp.zeros((D_MODEL, pad), F32)], axis=1).astype(BF16)
    zpad = jnp.zeros((pad,), F32)
    bias = jnp.concatenate([ssd_dt_bias[layer].reshape(-1), mlstm_gate_bias[layer].reshape(-1), zpad])
    a_log = jnp.concatenate([ssd_a_log[layer].reshape(-1), jnp.zeros((SMALL_COLS - 2 * SSD_HEADS,), F32)])
    par_row = jnp.zeros((8, SMALL_COLS), F32).at[0].set(bias).at[1].set(a_log)
    pg = jnp.zeros((8, D_MODEL), F32)
    pg = pg.at[0].set(jnp.repeat(ssd_d[layer], SSD_HEAD_DIM)).at[1].set(ssd_norm_g[layer])
    pg = pg.at[2].set(mlstm_norm_g[layer]).at[3].set(ret_norm_g[layer])
    return dict(main=main, small=small, par_row=par_row, par_col=par_row.T,
                w_gate=w_gate[layer].astype(BF16), b_gate=b_gate[layer].reshape(1, -1),
                conv_w=conv_w[layer], conv_b=conv_b[layer].reshape(1, -1), pg=pg,
                w_branch=w_branch[layer].astype(BF16), w_out=w_out[layer].astype(BF16),
                w_ffn_gate=w_ffn_gate[layer].astype(BF16), w_ffn_up=w_ffn_up[layer].astype(BF16),
                w_ffn_down=w_ffn_down[layer].astype(BF16))


def _rope_tables(s):
    half = RET_QK // 2
    inv = ROPE_BASE ** (-jnp.arange(half, dtype=F32) / half)
    ang = jnp.arange(s, dtype=F32)[:, None] * inv
    cos, sin = jnp.cos(ang), jnp.sin(ang)
    return jnp.concatenate([cos, cos], axis=1), jnp.concatenate([-sin, sin], axis=1)


def _trunk(x, layers, norm_mix_g, norm_ffn_g, final_norm_g):
    bsz, s, d = x.shape
    t = bsz * s
    cos_f, sin_f = _rope_tables(s)
    x2d = x.reshape(t, d)
    depth = len(layers)
    zero_bias = jnp.zeros((1, MAIN_COLS), F32)
    for li, p in enumerate(layers):
        g_mix = norm_mix_g[li].reshape(1, d)
        main2d = norm_matmul(x2d, g_mix, p["main"], zero_bias, "none", BF16, 1024, 512)
        gate = norm_matmul(x2d, g_mix, p["w_gate"], p["b_gate"], "sigmoid", BF16, 1024, 512)
        small, small_t = small_proj(x2d, g_mix, p["small"], 1024)
        main = main2d.reshape(bsz, s, MAIN_COLS)
        small = small.reshape(bsz, s, SMALL_COLS)
        xbc = conv_silu(main, p["conv_w"], p["conv_b"], 512)
        ssd_y = ssd_scan(xbc, small, small_t, p["par_row"], p["par_col"])
        ml_h = mlstm_scan(main, small, small_t, p["par_row"], p["par_col"])
        ret_y = retention_scan(main, cos_f, sin_f)
        flat = lambda pair: [a.reshape(t, d) for a in pair]
        x2d = post_mix(main2d, xbc.reshape(t, SSD_CONV_CH), flat(ssd_y), flat(ml_h), flat(ret_y),
                       gate, x2d, p["pg"], p["w_branch"], p["w_out"], 256)
        x2d = ffn(x2d, norm_ffn_g[li].reshape(1, d), p["w_ffn_gate"], p["w_ffn_up"], p["w_ffn_down"],
                  final_norm_g.reshape(1, d), li == depth - 1, 1024, 256)
    return x2d.reshape(bsz, s, d)


def kernel(x_prompt, x_sample, norm_mix_g, w_in, w_gate, b_gate, conv_w, conv_b, ssd_a_log, ssd_dt_bias, ssd_d, ssd_norm_g, mlstm_gate_bias, mlstm_norm_g, ret_norm_g, w_branch, w_out, norm_ffn_g, w_ffn_gate, w_ffn_up, w_ffn_down, final_norm_g):
    depth = w_in.shape[0]
    layers = [_prepare_layer(li, w_in, w_gate, b_gate, conv_w, conv_b, ssd_a_log, ssd_dt_bias, ssd_d,
                             ssd_norm_g, mlstm_gate_bias, mlstm_norm_g, ret_norm_g, w_branch, w_out,
                             w_ffn_gate, w_ffn_up, w_ffn_down) for li in range(depth)]
    y_prompt = _trunk(x_prompt, layers, norm_mix_g, norm_ffn_g, final_norm_g)
    y_sample = _trunk(x_sample, layers, norm_mix_g, norm_ffn_g, final_norm_g)
    return (y_prompt, y_sample)
```

```python
import functools
import math

import numpy as np
import jax
import jax.numpy as jnp
from jax import lax
from jax.experimental import pallas as pl
from jax.experimental.pallas import tpu as pltpu

F32 = jnp.float32
BF16 = jnp.bfloat16

D_MODEL = 1024
CHUNK = 128
EPS = 1e-6
SSD_HEADS = 16
SSD_HEAD_DIM = 64
SSD_DIM = SSD_HEADS * SSD_HEAD_DIM
SSD_GROUPS = 2
SSD_STATE = 128
SSD_CONV = 4
SSD_BC = 2 * SSD_GROUPS * SSD_STATE
SSD_CONV_CH = SSD_DIM + SSD_BC
ML_HEADS = 4
ML_QK = 128
ML_V = 256
RET_HEADS = 4
RET_QK = 128
RET_V = 256
ROPE_BASE = 10000.0
IN_SIZES = (SSD_DIM, SSD_CONV_CH, 2 * SSD_HEADS,
            ML_HEADS * ML_QK, ML_HEADS * ML_QK, ML_HEADS * ML_V, ML_HEADS * ML_V, 4 * ML_HEADS,
            RET_HEADS * RET_QK, RET_HEADS * RET_QK, RET_HEADS * RET_V, RET_HEADS * RET_V)

COL_Z, COL_MLV, COL_MLO, COL_RV, COL_RG, COL_XS = 0, 1024, 2048, 3072, 4096, 5120
COL_MLQ, COL_MLK, COL_RQ, COL_RK, COL_BC = 6144, 6656, 7168, 7680, 8192
MAIN_COLS = 8704

PREP_ROWS = 128
PREP_DT = 0
PREP_CS = 32
PREP_B = 64
PREP_A = 72
PREP_AMAX = 80
PREP_WST = 88
PREP_IN = 2 * SSD_HEADS + 4 * ML_HEADS
LOG2E = 1.4426950408889634

V7X_VMEM_BUDGET = 56 * 1024 * 1024
CONV_HALO = 16
BF16_SUBLANES = 16

_lin = np.linspace(math.log(1.0 / 32.0), math.log(1.0 / 512.0), RET_HEADS, dtype=np.float32)
RET_LOG_GAMMA = [float(v) for v in np.log(np.float32(1.0) - np.exp(_lin)).astype(np.float32)]


def _vmem_limit(nbytes):
    return int(min(V7X_VMEM_BUDGET, max(32 * 1024 * 1024, 2 * nbytes)))


def _silu(x):
    return x * (1.0 / (1.0 + jnp.exp(-x)))


def _sigmoid(x):
    return 1.0 / (1.0 + jnp.exp(-x))


def _softplus(x):
    return jnp.maximum(x, 0.0) + jnp.log1p(jnp.exp(-jnp.abs(x)))


def _log_sigmoid(x):
    return -_softplus(-x)


def _rms(x, g):
    return x * lax.rsqrt(jnp.mean(x * x, axis=-1, keepdims=True) + EPS) * g


def _dot(a, b):
    return jnp.dot(a, b, preferred_element_type=F32)


def _dot_nt(a, b):
    return lax.dot_general(a, b, (((1,), (1,)), ((), ())), preferred_element_type=F32)


def _dot_tn(a, b):
    return lax.dot_general(a, b, (((0,), (0,)), ((), ())), preferred_element_type=F32)


def _split3(v):
    hi = v.astype(BF16)
    r = v - hi.astype(F32)
    mid = r.astype(BF16)
    lo = (r - mid.astype(F32)).astype(BF16)
    return hi, mid, lo


def _tri_masks(n):
    r = lax.broadcasted_iota(jnp.int32, (n, n), 0)
    c = lax.broadcasted_iota(jnp.int32, (n, n), 1)
    return r >= c, c >= r


def _norm_matmul_kernel(x_ref, g_ref, w_ref, b_ref, o_ref, u_ref, *, act):
    @pl.when(pl.program_id(1) == 0)
    def _():
        u_ref[...] = _rms(x_ref[...], g_ref[...]).astype(BF16)

    acc = _dot(u_ref[...], w_ref[...]) + b_ref[...]
    if act == "sigmoid":
        acc = _sigmoid(acc)
    o_ref[...] = acc.astype(o_ref.dtype)


def norm_matmul(x2d, g, w, b, act, out_dtype, tm, tn):
    t, d = x2d.shape
    n = w.shape[1]
    tm = min(tm, t)
    est = 2 * tm * d * 4 + tm * d * 2 + 2 * d * tn * 2 + 2 * tm * tn * 2 + 2 * tm * tn * 4
    return pl.pallas_call(
        functools.partial(_norm_matmul_kernel, act=act),
        grid=(t // tm, n // tn),
        in_specs=[pl.BlockSpec((tm, d), lambda i, j: (i, 0)),
                  pl.BlockSpec((1, d), lambda i, j: (0, 0)),
                  pl.BlockSpec((d, tn), lambda i, j: (0, j)),
                  pl.BlockSpec((1, tn), lambda i, j: (0, j))],
        out_specs=pl.BlockSpec((tm, tn), lambda i, j: (i, j)),
        out_shape=jax.ShapeDtypeStruct((t, n), out_dtype),
        scratch_shapes=[pltpu.VMEM((tm, d), BF16)],
        compiler_params=pltpu.CompilerParams(dimension_semantics=("parallel", "arbitrary"),
                                             vmem_limit_bytes=_vmem_limit(est)),
        name="norm_matmul_" + act,
    )(x2d, g, w, b)


def _gate_prep_kernel(x_ref, g_ref, wt_ref, pcol_ref, row_ref, col_ref, *, tm):
    n = CHUNK
    u = _rms(x_ref[...], g_ref[...]).astype(BF16)
    pre = _dot_nt(wt_ref[...], u) + pcol_ref[:, 0:1]
    nd = 2 * SSD_HEADS
    dt = _softplus(pre[0:nd])
    da = dt * (-jnp.exp(pcol_ref[0:nd, 1:2])) * LOG2E
    ig = pre[nd:nd + 2 * ML_HEADS] * LOG2E
    lf = _log_sigmoid(pre[nd + 2 * ML_HEADS:nd + 4 * ML_HEADS]) * LOG2E
    stack = jnp.concatenate([da, lf, jnp.zeros((2 * ML_HEADS, tm), F32)], axis=0)
    ns = stack.shape[0]
    s3 = jnp.concatenate(_split3(stack), axis=0)
    lower, upper = _tri_masks(n)
    prefix = upper.astype(BF16)
    suffix = lower.astype(BF16)
    row_d = lax.broadcasted_iota(jnp.int32, (nd, n), 0)
    row_g = lax.broadcasted_iota(jnp.int32, (2 * ML_HEADS, n), 0)
    lane_g = lax.broadcasted_iota(jnp.int32, (2 * ML_HEADS, n), 1)
    pad = jnp.zeros((PREP_ROWS - PREP_WST - nd, n), F32)
    blocks = []
    for c in range(tm // n):
        sl = slice(c * n, (c + 1) * n)
        p3 = _dot(s3[:, sl], prefix)
        q3 = _dot(s3[:, sl], suffix)
        pf = p3[0:ns] + p3[ns:2 * ns] + p3[2 * ns:3 * ns]
        sf = q3[0:ns] + q3[ns:2 * ns] + q3[2 * ns:3 * ns]
        is_fwd = row_d < SSD_HEADS
        cs = jnp.where(is_fwd, pf[0:nd], sf[0:nd])
        cs_end = jnp.where(is_fwd, cs[:, n - 1:n], cs[:, 0:1])
        wst = dt[:, sl] * jnp.exp2(cs_end - cs)
        b = jnp.where(row_g < ML_HEADS, pf[nd:nd + 2 * ML_HEADS], sf[nd:nd + 2 * ML_HEADS])
        a = ig[:, sl] - b
        pm, sm = a, a
        for sh in (1, 2, 4, 8, 16, 32, 64):
            pm = jnp.maximum(pm, jnp.where(lane_g >= sh, pltpu.roll(pm, sh, 1), -jnp.inf))
            sm = jnp.maximum(sm, jnp.where(lane_g < n - sh, pltpu.roll(sm, n - sh, 1), -jnp.inf))
        amax = jnp.where(row_g < ML_HEADS, pm, sm)
        blocks.append(jnp.concatenate([dt[:, sl], cs, b, a, amax, wst, pad], axis=0))
    rows = jnp.concatenate(blocks, axis=1)
    row_ref[...] = rows
    col_ref[...] = rows.T


def gate_prep(x2d, g, w_small_t, par_col, tm):
    t, d = x2d.shape
    tm = min(tm, t)
    return pl.pallas_call(
        functools.partial(_gate_prep_kernel, tm=tm),
        grid=(t // tm,),
        in_specs=[pl.BlockSpec((tm, d), lambda i: (i, 0)),
                  pl.BlockSpec((1, d), lambda i: (0, 0)),
                  pl.BlockSpec((PREP_ROWS, d), lambda i: (0, 0)),
                  pl.BlockSpec((PREP_ROWS, 8), lambda i: (0, 0))],
        out_specs=[pl.BlockSpec((PREP_ROWS, tm), lambda i: (0, i)),
                   pl.BlockSpec((tm, PREP_ROWS), lambda i: (i, 0))],
        out_shape=[jax.ShapeDtypeStruct((PREP_ROWS, t), F32),
                   jax.ShapeDtypeStruct((t, PREP_ROWS), F32)],
        compiler_params=pltpu.CompilerParams(dimension_semantics=("parallel",),
                                             vmem_limit_bytes=_vmem_limit(4 * tm * d * 4)),
        name="gate_prep",
    )(x2d, g, w_small_t, par_col)


def _conv_kernel(xs_ref, bc_ref, xs_p, bc_p, xs_n, bc_n, w_ref, b_ref, o_ref, ext_ref, *, tc, nt):
    t = pl.program_id(1)
    h = CONV_HALO
    has_prev = t > 0
    has_next = t < nt - 1
    zeros_x = jnp.zeros((h, SSD_DIM), F32)
    zeros_b = jnp.zeros((h, SSD_BC), F32)
    ext_ref[0:h, 0:SSD_DIM] = jnp.where(has_prev, xs_p[0].astype(F32), zeros_x)
    ext_ref[0:h, SSD_DIM:] = jnp.where(has_prev, bc_p[0].astype(F32), zeros_b)
    ext_ref[h:h + tc, 0:SSD_DIM] = xs_ref[0].astype(F32)
    ext_ref[h:h + tc, SSD_DIM:] = bc_ref[0].astype(F32)
    ext_ref[h + tc:, 0:SSD_DIM] = jnp.where(has_next, xs_n[0].astype(F32), zeros_x)
    ext_ref[h + tc:, SSD_DIM:] = jnp.where(has_next, bc_n[0].astype(F32), zeros_b)
    left = SSD_CONV // 2
    acc = jnp.broadcast_to(b_ref[...], (tc, SSD_CONV_CH))
    for tap in range(SSD_CONV):
        acc = acc + ext_ref[pl.ds(h - left + tap, tc), :] * w_ref[tap:tap + 1, :]
    o_ref[0] = _silu(acc).astype(o_ref.dtype)


def conv_silu(main, conv_w, conv_b, tc):
    bsz, s, _ = main.shape
    tc = min(tc, s)
    nt = s // tc
    hb = tc // CONV_HALO
    last_hb = s // CONV_HALO - 1
    cx, cb = COL_XS // SSD_DIM, COL_BC // SSD_BC
    prev_map = lambda b, t: (b, jnp.maximum(t * hb - 1, 0))
    next_map = lambda b, t: (b, jnp.minimum((t + 1) * hb, last_hb))
    est = 2 * tc * SSD_CONV_CH * 2 * 2 + (tc + 2 * CONV_HALO) * SSD_CONV_CH * 4 * 3
    return pl.pallas_call(
        functools.partial(_conv_kernel, tc=tc, nt=nt),
        grid=(bsz, nt),
        in_specs=[pl.BlockSpec((1, tc, SSD_DIM), lambda b, t: (b, t, cx)),
                  pl.BlockSpec((1, tc, SSD_BC), lambda b, t: (b, t, cb)),
                  pl.BlockSpec((1, CONV_HALO, SSD_DIM), lambda b, t: prev_map(b, t) + (cx,)),
                  pl.BlockSpec((1, CONV_HALO, SSD_BC), lambda b, t: prev_map(b, t) + (cb,)),
                  pl.BlockSpec((1, CONV_HALO, SSD_DIM), lambda b, t: next_map(b, t) + (cx,)),
                  pl.BlockSpec((1, CONV_HALO, SSD_BC), lambda b, t: next_map(b, t) + (cb,)),
                  pl.BlockSpec((SSD_CONV, SSD_CONV_CH), lambda b, t: (0, 0)),
                  pl.BlockSpec((1, SSD_CONV_CH), lambda b, t: (0, 0))],
        out_specs=pl.BlockSpec((1, tc, SSD_CONV_CH), lambda b, t: (b, t, 0)),
        out_shape=jax.ShapeDtypeStruct((bsz, s, SSD_CONV_CH), BF16),
        scratch_shapes=[pltpu.VMEM((tc + 2 * CONV_HALO, SSD_CONV_CH), F32)],
        compiler_params=pltpu.CompilerParams(dimension_semantics=("parallel", "parallel"),
                                             vmem_limit_bytes=_vmem_limit(est)),
        name="conv_silu",
    )(main, main, main, main, main, main, conv_w, conv_b)


def _ssd_dir(d, xbc, col, row, p_ref, y_ref, lower, upper):
    n = CHUNK
    mask = lower if d == 0 else upper
    last = n - 1 if d == 0 else 0
    c0 = d * SSD_HEADS
    hpg = SSD_HEADS // SSD_GROUPS
    gw = hpg * SSD_HEAD_DIM

    lane_c = lax.broadcasted_iota(jnp.int32, (n, PREP_ROWS), 1)
    is_cs = (lane_c >= PREP_CS + c0) & (lane_c < PREP_CS + c0 + SSD_HEADS)
    z = jnp.where(is_cs, jnp.exp2(jnp.where(is_cs, col, 0.0)), col)
    t0 = n - BF16_SUBLANES if d == 0 else 0
    zt = z[t0:t0 + BF16_SUBLANES]
    zt_hi = zt.astype(BF16)
    zt_mid = (zt - zt_hi.astype(F32)).astype(BF16)
    lhs = jnp.concatenate([z.astype(BF16), zt_hi, zt_mid], axis=0)
    er = lax.broadcasted_iota(jnp.int32, (PREP_ROWS, 2 * SSD_DIM), 0)
    ec = lax.broadcasted_iota(jnp.int32, (PREP_ROWS, 2 * SSD_DIM), 1)
    src = jnp.where(ec < SSD_DIM, PREP_WST + c0 + ec // SSD_HEAD_DIM,
                    PREP_CS + c0 + (ec - SSD_DIM) // SSD_HEAD_DIM)
    res = _dot(lhs, (er == src).astype(BF16))
    w_state_x = res[0:n, 0:SSD_DIM]
    e_cs_x = res[0:n, SSD_DIM:]
    cd_tile = res[n:n + BF16_SUBLANES, SSD_DIM:] + res[n + BF16_SUBLANES:, SSD_DIM:]
    r = last - t0
    chunk_decay_x = cd_tile[r:r + 1, :]

    dt_t = row[PREP_DT + c0:PREP_DT + c0 + SSD_HEADS, :]
    r2_t = row[PREP_CS + c0:PREP_CS + c0 + SSD_HEADS, :] - jnp.log2(dt_t)

    xs = xbc[:, 0:SSD_DIM]
    xw = (xs.astype(F32) * w_state_x).astype(BF16)
    lane = lax.broadcasted_iota(jnp.int32, (n, 2 * SSD_HEAD_DIM), 1)
    for g in range(SSD_GROUPS):
        bm = xbc[:, SSD_DIM + g * SSD_STATE:SSD_DIM + (g + 1) * SSD_STATE]
        cm = xbc[:, SSD_DIM + (SSD_GROUPS + g) * SSD_STATE:SSD_DIM + (SSD_GROUPS + g + 1) * SSD_STATE]
        cb = _dot_nt(cm, bm)
        p_prev = p_ref[d, g]
        y_off = _dot(cm, p_prev.astype(BF16)) * e_cs_x[:, g * gw:(g + 1) * gw]
        p_ref[d, g] = p_prev * chunk_decay_x[:, g * gw:(g + 1) * gw] + _dot_tn(bm, xw[:, g * gw:(g + 1) * gw])
        for j in range(hpg // 2):
            h0 = g * hpg + 2 * j
            ms = []
            for hh in (h0, h0 + 1):
                seg = col[:, PREP_CS + c0 + hh:PREP_CS + c0 + hh + 1] - r2_t[hh:hh + 1, :]
                ms.append((cb * jnp.exp2(jnp.where(mask, seg, -jnp.inf))).astype(BF16))
            lhs_p = jnp.concatenate(ms, axis=1)
            xp = xs[:, h0 * SSD_HEAD_DIM:(h0 + 2) * SSD_HEAD_DIM]
            zero = jnp.zeros_like(xp)
            rhs = jnp.concatenate([jnp.where(lane < SSD_HEAD_DIM, xp, zero),
                                   jnp.where(lane >= SSD_HEAD_DIM, xp, zero)], axis=0)
            lo = 2 * j * SSD_HEAD_DIM
            y_ref[0, :, h0 * SSD_HEAD_DIM:(h0 + 2) * SSD_HEAD_DIM] = (
                _dot(lhs_p, rhs) + y_off[:, lo:lo + 2 * SSD_HEAD_DIM]).astype(y_ref.dtype)


def _ssd_kernel(xf_ref, xb_ref, cf_ref, cb_ref, rf_ref, rb_ref, yf_ref, yb_ref, p_ref):
    @pl.when(pl.program_id(1) == 0)
    def _():
        p_ref[...] = jnp.zeros_like(p_ref)

    lower, upper = _tri_masks(CHUNK)
    _ssd_dir(0, xf_ref[0], cf_ref[0], rf_ref[...], p_ref, yf_ref, lower, upper)
    _ssd_dir(1, xb_ref[0], cb_ref[0], rb_ref[...], p_ref, yb_ref, lower, upper)


def _scan_maps(nc):
    fwd3 = lambda b, c: (b, c, 0)
    bwd3 = lambda b, c: (b, nc - 1 - c, 0)
    fwd_t = lambda b, c: (0, b * nc + c)
    bwd_t = lambda b, c: (0, b * nc + nc - 1 - c)
    return fwd3, bwd3, fwd_t, bwd_t


def ssd_scan(xbc, prep_col, prep_row):
    bsz, s, _ = xbc.shape
    nc = s // CHUNK
    fwd3, bwd3, fwd_t, bwd_t = _scan_maps(nc)
    hpg = SSD_HEADS // SSD_GROUPS
    out = jax.ShapeDtypeStruct((bsz, s, SSD_DIM), BF16)
    return pl.pallas_call(
        _ssd_kernel,
        grid=(bsz, nc),
        in_specs=[pl.BlockSpec((1, CHUNK, SSD_CONV_CH), fwd3),
                  pl.BlockSpec((1, CHUNK, SSD_CONV_CH), bwd3),
                  pl.BlockSpec((1, CHUNK, PREP_ROWS), fwd3),
                  pl.BlockSpec((1, CHUNK, PREP_ROWS), bwd3),
                  pl.BlockSpec((PREP_ROWS, CHUNK), fwd_t),
                  pl.BlockSpec((PREP_ROWS, CHUNK), bwd_t)],
        out_specs=[pl.BlockSpec((1, CHUNK, SSD_DIM), fwd3),
                   pl.BlockSpec((1, CHUNK, SSD_DIM), bwd3)],
        out_shape=[out, out],
        scratch_shapes=[pltpu.VMEM((2, SSD_GROUPS, SSD_STATE, hpg * SSD_HEAD_DIM), F32)],
        compiler_params=pltpu.CompilerParams(dimension_semantics=("parallel", "arbitrary"),
                                             vmem_limit_bytes=_vmem_limit(16 * 1024 * 1024)),
        name="ssd_scan",
    )(xbc, xbc, prep_col, prep_col, prep_row, prep_row)


def _mlstm_dir(d, q_ref, k_ref, v_ref, col, row, s_ref, m_ref, h_ref, lower, upper):
    n = CHUNK
    mask = lower if d == 0 else upper
    last = n - 1 if d == 0 else 0
    ones = jnp.ones((n, ML_QK), BF16)
    for h in range(ML_HEADS):
        idx = d * ML_HEADS + h
        a_row = row[PREP_A + idx:PREP_A + idx + 1, :]
        a_col = col[:, PREP_A + idx:PREP_A + idx + 1]
        b_col = col[:, PREP_B + idx:PREP_B + idx + 1]
        amax_col = col[:, PREP_AMAX + idx:PREP_AMAX + idx + 1]
        m_prev = m_ref[idx:idx + 1, 0:1]
        g_col = jnp.maximum(m_prev, amax_col)
        w = jnp.exp2(jnp.where(mask, a_row - g_col, -jnp.inf))
        w_inter = jnp.exp2(m_prev - g_col)
        qh = (q_ref[0, :, h * ML_QK:(h + 1) * ML_QK].astype(F32) * (ML_QK ** -0.5)).astype(BF16)
        kh = k_ref[0, :, h * ML_QK:(h + 1) * ML_QK]
        vx = jnp.concatenate([v_ref[0, :, h * ML_V:(h + 1) * ML_V], ones], axis=1)
        scores = (_dot_nt(qh, kh) * w).astype(BF16)
        state = s_ref[idx]
        tot = _dot(scores, vx) + w_inter * _dot(qh, state.astype(BF16))
        den = jnp.maximum(jnp.abs(tot[:, ML_V:]), jnp.exp2(-(b_col + g_col)))
        rinv = 1.0 / den
        h_ref[0, :, h * ML_V:(h + 1) * ML_V] = (
            tot[:, 0:ML_V] * jnp.concatenate([rinv, rinv], axis=1)).astype(h_ref.dtype)
        g_last = g_col[last:last + 1, :]
        w_end = jnp.exp2(a_col - g_last)
        dec = jnp.exp2(m_prev - g_last)
        kw = (kh.astype(F32) * w_end).astype(BF16)
        s_ref[idx] = dec * state + _dot_tn(kw, vx)
        m_ref[idx:idx + 1, :] = jnp.broadcast_to(b_col[last:last + 1, :] + g_last, (1, PREP_ROWS))


def _mlstm_kernel(qf_ref, kf_ref, vf_ref, qb_ref, kb_ref, vb_ref, cf_ref, cb_ref, rf_ref, rb_ref,
                  hf_ref, hb_ref, s_ref, m_ref):
    @pl.when(pl.program_id(1) == 0)
    def _():
        s_ref[...] = jnp.zeros_like(s_ref)
        m_ref[...] = jnp.zeros_like(m_ref)

    lower, upper = _tri_masks(CHUNK)
    _mlstm_dir(0, qf_ref, kf_ref, vf_ref, cf_ref[0], rf_ref[...], s_ref, m_ref, hf_ref, lower, upper)
    _mlstm_dir(1, qb_ref, kb_ref, vb_ref, cb_ref[0], rb_ref[...], s_ref, m_ref, hb_ref, lower, upper)


def mlstm_scan(main, prep_col, prep_row):
    bsz, s, _ = main.shape
    nc = s // CHUNK
    fwd3, bwd3, fwd_t, bwd_t = _scan_maps(nc)
    qk_w, v_w = ML_HEADS * ML_QK, ML_HEADS * ML_V
    cq, ck, cv = COL_MLQ // qk_w, COL_MLK // qk_w, COL_MLV // v_w

    def col(m, cidx):
        return lambda b, c: m(b, c)[:2] + (cidx,)

    out = jax.ShapeDtypeStruct((bsz, s, v_w), BF16)
    return pl.pallas_call(
        _mlstm_kernel,
        grid=(bsz, nc),
        in_specs=[pl.BlockSpec((1, CHUNK, qk_w), col(fwd3, cq)),
                  pl.BlockSpec((1, CHUNK, qk_w), col(fwd3, ck)),
                  pl.BlockSpec((1, CHUNK, v_w), col(fwd3, cv)),
                  pl.BlockSpec((1, CHUNK, qk_w), col(bwd3, cq)),
                  pl.BlockSpec((1, CHUNK, qk_w), col(bwd3, ck)),
                  pl.BlockSpec((1, CHUNK, v_w), col(bwd3, cv)),
                  pl.BlockSpec((1, CHUNK, PREP_ROWS), fwd3),
                  pl.BlockSpec((1, CHUNK, PREP_ROWS), bwd3),
                  pl.BlockSpec((PREP_ROWS, CHUNK), fwd_t),
                  pl.BlockSpec((PREP_ROWS, CHUNK), bwd_t)],
        out_specs=[pl.BlockSpec((1, CHUNK, v_w), fwd3),
                   pl.BlockSpec((1, CHUNK, v_w), bwd3)],
        out_shape=[out, out],
        scratch_shapes=[pltpu.VMEM((2 * ML_HEADS, ML_QK, ML_V + ML_QK), F32),
                        pltpu.VMEM((2 * ML_HEADS, PREP_ROWS), F32)],
        compiler_params=pltpu.CompilerParams(dimension_semantics=("parallel", "arbitrary"),
                                             vmem_limit_bytes=_vmem_limit(16 * 1024 * 1024)),
        name="mlstm_scan",
    )(main, main, main, main, main, main, prep_col, prep_col, prep_row, prep_row)


def _rotate(t, cos_f, sin_f):
    return t * cos_f + pltpu.roll(t, RET_QK // 2, 1) * sin_f


def _ret_constants(dm_ref, xz_ref):
    n = CHUNK
    row = lax.broadcasted_iota(jnp.int32, (n, n), 0)
    colm = lax.broadcasted_iota(jnp.int32, (n, n), 1)
    dist = (row - colm).astype(F32)
    pos = lax.broadcasted_iota(jnp.int32, (n, RET_QK), 0).astype(F32)
    for d in range(2):
        for h in range(RET_HEADS):
            lg = RET_LOG_GAMMA[h]
            idx = d * RET_HEADS + h
            if d == 0:
                dm_ref[idx] = jnp.where(dist >= 0, jnp.exp(jnp.maximum(dist, 0.0) * lg), 0.0)
                xz_ref[idx, 0:n, :] = jnp.exp((pos + 1.0) * lg)
                xz_ref[idx, n:, :] = jnp.exp((n - 1.0 - pos) * lg)
            else:
                dm_ref[idx] = jnp.where(dist < 0, jnp.exp(jnp.maximum(-dist, 0.0) * lg), 0.0)
                xz_ref[idx, 0:n, :] = jnp.exp((n - pos) * lg)
                xz_ref[idx, n:, :] = jnp.exp(pos * lg)


def _ret_dir(d, q_ref, k_ref, v_ref, cos_ref, sin_ref, r_ref, dm_ref, xz_ref, y_ref):
    n = CHUNK
    cos_f = cos_ref[...]
    sin_f = sin_ref[...]
    for h in range(RET_HEADS):
        idx = d * RET_HEADS + h
        xi = xz_ref[idx, 0:n, :]
        zeta = xz_ref[idx, n:, :]
        qf = _rotate(q_ref[0, :, h * RET_QK:(h + 1) * RET_QK].astype(F32), cos_f, sin_f)
        kf = _rotate(k_ref[0, :, h * RET_QK:(h + 1) * RET_QK].astype(F32), cos_f, sin_f) * (RET_QK ** -0.5)
        qh = qf.astype(BF16)
        kh = kf.astype(BF16)
        vh = v_ref[0, :, h * RET_V:(h + 1) * RET_V]
        scores = (_dot_nt(qh, kh) * dm_ref[idx]).astype(BF16)
        r_prev = r_ref[idx]
        cross = _dot(qh, r_prev.astype(BF16)) * jnp.concatenate([xi, xi], axis=1)
        y_ref[0, :, h * RET_V:(h + 1) * RET_V] = (_dot(scores, vh) + cross).astype(y_ref.dtype)
        r_ref[idx] = r_prev * math.exp(n * RET_LOG_GAMMA[h]) + _dot_tn((kf * zeta).astype(BF16), vh)


def _ret_kernel(qf_ref, kf_ref, vf_ref, cf_ref, nf_ref, qb_ref, kb_ref, vb_ref, cb_ref, nb_ref,
                yf_ref, yb_ref, r_ref, dm_ref, xz_ref):
    @pl.when(pl.program_id(1) == 0)
    def _():
        r_ref[...] = jnp.zeros_like(r_ref)
        _ret_constants(dm_ref, xz_ref)

    _ret_dir(0, qf_ref, kf_ref, vf_ref, cf_ref, nf_ref, r_ref, dm_ref, xz_ref, yf_ref)
    _ret_dir(1, qb_ref, kb_ref, vb_ref, cb_ref, nb_ref, r_ref, dm_ref, xz_ref, yb_ref)


def retention_scan(main, cos_f, sin_f):
    bsz, s, _ = main.shape
    nc = s // CHUNK
    fwd3, bwd3, _, _ = _scan_maps(nc)
    qk_w, v_w = RET_HEADS * RET_QK, RET_HEADS * RET_V
    cq, ck, cv = COL_RQ // qk_w, COL_RK // qk_w, COL_RV // v_w

    def col(m, cidx):
        return lambda b, c: m(b, c)[:2] + (cidx,)

    tab_f = lambda b, c: (c, 0)
    tab_b = lambda b, c: (nc - 1 - c, 0)
    out = jax.ShapeDtypeStruct((bsz, s, v_w), BF16)
    return pl.pallas_call(
        _ret_kernel,
        grid=(bsz, nc),
        in_specs=[pl.BlockSpec((1, CHUNK, qk_w), col(fwd3, cq)),
                  pl.BlockSpec((1, CHUNK, qk_w), col(fwd3, ck)),
                  pl.BlockSpec((1, CHUNK, v_w), col(fwd3, cv)),
                  pl.BlockSpec((CHUNK, RET_QK), tab_f),
                  pl.BlockSpec((CHUNK, RET_QK), tab_f),
                  pl.BlockSpec((1, CHUNK, qk_w), col(bwd3, cq)),
                  pl.BlockSpec((1, CHUNK, qk_w), col(bwd3, ck)),
                  pl.BlockSpec((1, CHUNK, v_w), col(bwd3, cv)),
                  pl.BlockSpec((CHUNK, RET_QK), tab_b),
                  pl.BlockSpec((CHUNK, RET_QK), tab_b)],
        out_specs=[pl.BlockSpec((1, CHUNK, v_w), fwd3),
                   pl.BlockSpec((1, CHUNK, v_w), bwd3)],
        out_shape=[out, out],
        scratch_shapes=[pltpu.VMEM((2 * RET_HEADS, RET_QK, RET_V), F32),
                        pltpu.VMEM((2 * RET_HEADS, CHUNK, CHUNK), F32),
                        pltpu.VMEM((2 * RET_HEADS, 2 * CHUNK, RET_QK), F32)],
        compiler_params=pltpu.CompilerParams(dimension_semantics=("parallel", "arbitrary"),
                                             vmem_limit_bytes=_vmem_limit(16 * 1024 * 1024)),
        name="retention_scan",
    )(main, main, main, cos_f, sin_f, main, main, main, cos_f, sin_f)


def _head_norm(y, nheads, width):
    parts = []
    for h in range(nheads):
        yh = y[:, h * width:(h + 1) * width]
        parts.append(yh * lax.rsqrt(jnp.mean(yh * yh, axis=-1, keepdims=True) + EPS))
    return jnp.concatenate(parts, axis=1)


def _post_mix_kernel(z_ref, o_ref, rg_ref, xs_ref, sf_ref, sb_ref, mf_ref, mb_ref, rf_ref, rb_ref,
                     gate_ref, x_ref, pg_ref, wb_ref, wo_ref, out_ref):
    d_skip, g_ssd, g_ml, g_ret = pg_ref[0:1, :], pg_ref[1:2, :], pg_ref[2:3, :], pg_ref[3:4, :]
    f32 = lambda ref: ref[...].astype(F32)
    ya = (f32(sf_ref) + f32(sb_ref) + f32(xs_ref) * d_skip) * _silu(f32(z_ref))
    ya = _rms(ya, g_ssd).astype(BF16)
    yb = _head_norm(f32(mf_ref) + f32(mb_ref), ML_HEADS, ML_V) * g_ml
    yb = (_sigmoid(f32(o_ref)) * yb).astype(BF16)
    yc = _head_norm(f32(rf_ref) + f32(rb_ref), RET_HEADS, RET_V) * g_ret
    yc = (_silu(f32(rg_ref)) * yc).astype(BF16)
    gate = f32(gate_ref)
    merged = (gate[:, 0:D_MODEL] * _dot(ya, wb_ref[0])
              + gate[:, D_MODEL:2 * D_MODEL] * _dot(yb, wb_ref[1])
              + gate[:, 2 * D_MODEL:] * _dot(yc, wb_ref[2]))
    out_ref[...] = x_ref[...] + _dot(merged.astype(BF16), wo_ref[...])


def post_mix(main2d, xbc2d, ssd_y, ml_h, ret_y, gate, x2d, pg, w_branch, w_out, tm):
    t, d = x2d.shape
    tm = min(tm, t)
    row = lambda i: (i, 0)
    blk = lambda c: pl.BlockSpec((tm, d), lambda i: (i, c))
    est = 2 * tm * d * (10 * 2 + 3 * 2 + 4 + 4) + 2 * 4 * d * d * 2 + 8 * tm * d * 4
    return pl.pallas_call(
        _post_mix_kernel,
        grid=(t // tm,),
        in_specs=[blk(COL_Z // d), blk(COL_MLO // d), blk(COL_RG // d), blk(0),
                  blk(0), blk(0), blk(0), blk(0), blk(0), blk(0),
                  pl.BlockSpec((tm, 3 * d), row), blk(0),
                  pl.BlockSpec((8, d), lambda i: (0, 0)),
                  pl.BlockSpec((3, d, d), lambda i: (0, 0, 0)),
                  pl.BlockSpec((d, d), lambda i: (0, 0))],
        out_specs=pl.BlockSpec((tm, d), row),
        out_shape=jax.ShapeDtypeStruct((t, d), F32),
        compiler_params=pltpu.CompilerParams(dimension_semantics=("parallel",),
                                             vmem_limit_bytes=_vmem_limit(est)),
        name="post_mix",
    )(main2d, main2d, main2d, xbc2d, ssd_y[0], ssd_y[1], ml_h[0], ml_h[1], ret_y[0], ret_y[1],
      gate, x2d, pg, w_branch, w_out)


def _ffn_kernel(x_ref, g_ref, wg_ref, wu_ref, wd_ref, gf_ref, o_ref, h_ref, acc_ref, *, nf, final):
    j = pl.program_id(1)

    @pl.when(j == 0)
    def _():
        h_ref[...] = _rms(x_ref[...], g_ref[...]).astype(BF16)
        acc_ref[...] = jnp.zeros_like(acc_ref)

    h = h_ref[...]
    t = (_silu(_dot(h, wg_ref[...])) * _dot(h, wu_ref[...])).astype(BF16)
    acc_ref[...] += _dot(t, wd_ref[...])

    @pl.when(j == nf - 1)
    def _():
        y = x_ref[...] + acc_ref[...]
        if final:
            y = _rms(y, gf_ref[...])
        o_ref[...] = y


def ffn(x2d, g, w_gate, w_up, w_down, g_final, final, tm, tf):
    t, d = x2d.shape
    f = w_gate.shape[1]
    tm = min(tm, t)
    nf = f // tf
    est = 2 * 2 * tm * d * 4 + tm * d * 6 + 2 * 3 * d * tf * 2 + 3 * tm * tf * 4
    return pl.pallas_call(
        functools.partial(_ffn_kernel, nf=nf, final=final),
        grid=(t // tm, nf),
        in_specs=[pl.BlockSpec((tm, d), lambda i, j: (i, 0)),
                  pl.BlockSpec((1, d), lambda i, j: (0, 0)),
                  pl.BlockSpec((d, tf), lambda i, j: (0, j)),
                  pl.BlockSpec((d, tf), lambda i, j: (0, j)),
                  pl.BlockSpec((tf, d), lambda i, j: (j, 0)),
                  pl.BlockSpec((1, d), lambda i, j: (0, 0))],
        out_specs=pl.BlockSpec((tm, d), lambda i, j: (i, 0)),
        out_shape=jax.ShapeDtypeStruct((t, d), F32),
        scratch_shapes=[pltpu.VMEM((tm, d), BF16), pltpu.VMEM((tm, d), F32)],
        compiler_params=pltpu.CompilerParams(dimension_semantics=("parallel", "arbitrary"),
                                             vmem_limit_bytes=_vmem_limit(est)),
        name="ffn_final" if final else "ffn",
    )(x2d, g, w_gate, w_up, w_down, g_final)


def _prepare_layer(layer, w_in, w_gate, b_gate, conv_w, conv_b, ssd_a_log, ssd_dt_bias, ssd_d, ssd_norm_g,
                   mlstm_gate_bias, mlstm_norm_g, ret_norm_g, w_branch, w_out, w_ffn_gate, w_ffn_up,
                   w_ffn_down):
    offs = np.cumsum((0,) + IN_SIZES)
    w = w_in[layer]
    piece = lambda i: w[:, offs[i]:offs[i + 1]]
    z, xbc, dt, mlq, mlk, mlv, mlo, mlg, rq, rk, rv, rg = (piece(i) for i in range(len(IN_SIZES)))
    main = jnp.concatenate([z, mlv, mlo, rv, rg, xbc[:, :SSD_DIM], mlq, mlk, rq, rk, xbc[:, SSD_DIM:]],
                           axis=1).astype(BF16)
    hq = ML_HEADS
    gate_order = np.concatenate([np.arange(0, hq), np.arange(2 * hq, 3 * hq),
                                 np.arange(hq, 2 * hq), np.arange(3 * hq, 4 * hq)])
    pad = PREP_ROWS - PREP_IN
    small = jnp.concatenate([dt, mlg[:, gate_order], jnp.zeros((D_MODEL, pad), F32)], axis=1)
    bias = jnp.concatenate([ssd_dt_bias[layer].reshape(-1), mlstm_gate_bias[layer].reshape(-1)[gate_order],
                            jnp.zeros((pad,), F32)])
    a_log = jnp.concatenate([ssd_a_log[layer].reshape(-1), jnp.zeros((PREP_ROWS - 2 * SSD_HEADS,), F32)])
    par_col = jnp.zeros((PREP_ROWS, 8), F32).at[:, 0].set(bias).at[:, 1].set(a_log)
    pg = jnp.zeros((8, D_MODEL), F32)
    pg = pg.at[0].set(jnp.repeat(ssd_d[layer], SSD_HEAD_DIM)).at[1].set(ssd_norm_g[layer])
    pg = pg.at[2].set(mlstm_norm_g[layer]).at[3].set(ret_norm_g[layer])
    return dict(main=main, small_t=small.T.astype(BF16), par_col=par_col,
                w_gate=w_gate[layer].astype(BF16), b_gate=b_gate[layer].reshape(1, -1),
                conv_w=conv_w[layer], conv_b=conv_b[layer].reshape(1, -1), pg=pg,
                w_branch=w_branch[layer].astype(BF16), w_out=w_out[layer].astype(BF16),
                w_ffn_gate=w_ffn_gate[layer].astype(BF16), w_ffn_up=w_ffn_up[layer].astype(BF16),
                w_ffn_down=w_ffn_down[layer].astype(BF16))


def _rope_tables(s):
    half = RET_QK // 2
    inv = ROPE_BASE ** (-jnp.arange(half, dtype=F32) / half)
    ang = jnp.arange(s, dtype=F32)[:, None] * inv
    cos, sin = jnp.cos(ang), jnp.sin(ang)
    return jnp.concatenate([cos, cos], axis=1), jnp.concatenate([-sin, sin], axis=1)


def _trunk(x, layers, norm_mix_g, norm_ffn_g, final_norm_g):
    bsz, s, d = x.shape
    t = bsz * s
    cos_f, sin_f = _rope_tables(s)
    x2d = x.reshape(t, d)
    depth = len(layers)
    zero_bias = jnp.zeros((1, MAIN_COLS), F32)
    for li, p in enumerate(layers):
        g_mix = norm_mix_g[li].reshape(1, d)
        main2d = norm_matmul(x2d, g_mix, p["main"], zero_bias, "none", BF16, 1024, 2176)
        gate = norm_matmul(x2d, g_mix, p["w_gate"], p["b_gate"], "sigmoid", BF16, 1024, 1536)
        prep_row, prep_col = gate_prep(x2d, g_mix, p["small_t"], p["par_col"], 1024)
        main = main2d.reshape(bsz, s, MAIN_COLS)
        prep_col = prep_col.reshape(bsz, s, PREP_ROWS)
        xbc = conv_silu(main, p["conv_w"], p["conv_b"], 512)
        ssd_y = ssd_scan(xbc, prep_col, prep_row)
        ml_h = mlstm_scan(main, prep_col, prep_row)
        ret_y = retention_scan(main, cos_f, sin_f)
        flat = lambda pair: [a.reshape(t, d) for a in pair]
        x2d = post_mix(main2d, xbc.reshape(t, SSD_CONV_CH), flat(ssd_y), flat(ml_h), flat(ret_y),
                       gate, x2d, p["pg"], p["w_branch"], p["w_out"], 256)
        x2d = ffn(x2d, norm_ffn_g[li].reshape(1, d), p["w_ffn_gate"], p["w_ffn_up"], p["w_ffn_down"],
                  final_norm_g.reshape(1, d), li == depth - 1, 1024, 256)
    return x2d.reshape(bsz, s, d)


def kernel(x_prompt, x_sample, norm_mix_g, w_in, w_gate, b_gate, conv_w, conv_b, ssd_a_log, ssd_dt_bias, ssd_d, ssd_norm_g, mlstm_gate_bias, mlstm_norm_g, ret_norm_g, w_branch, w_out, norm_ffn_g, w_ffn_gate, w_ffn_up, w_ffn_down, final_norm_g):
    depth = w_in.shape[0]
    layers = [_prepare_layer(li, w_in, w_gate, b_gate, conv_w, conv_b, ssd_a_log, ssd_dt_bias, ssd_d,
                             ssd_norm_g, mlstm_gate_bias, mlstm_norm_g, ret_norm_g, w_branch, w_out,
                             w_ffn_gate, w_ffn_up, w_ffn_down) for li in range(depth)]
    y_prompt = _trunk(x_prompt, layers, norm_mix_g, norm_ffn_g, final_norm_g)
    y_sample = _trunk(x_sample, layers, norm_mix_g, norm_ffn_g, final_norm_g)
    return (y_prompt, y_sample)
```
